```python
import jax, jax.numpy as jnp
from jax import lax
import numpy as np


D_MODEL = 1024
BATCH = 2
SEQ = 8192
DEPTH = 1

N_HEADS_A = 8
N_HEADS_B = 8
HEAD_DIM = 64
ROPE_DIM = HEAD_DIM // 4
NOPE_DIM = HEAD_DIM - ROPE_DIM
KV_RANK = 128
IDX_HEADS = 8
IDX_DIM = 64
IDX_TOPK = 256
MOBA_BLOCK = 256
MOBA_TOPK = 3
Q_BLOCK = 128
D_FF = 2816
ROPE_THETA = 500000.0
EPS = 1e-6
N_MOD = 9
WIDTH_A = N_HEADS_A * HEAD_DIM
WIDTH_B = N_HEADS_B * HEAD_DIM
IN_SPLITS = (WIDTH_A, KV_RANK, ROPE_DIM, IDX_HEADS * IDX_DIM, IDX_DIM, IDX_HEADS, WIDTH_B, WIDTH_B, WIDTH_B, D_MODEL, D_MODEL)
IN_COLS = sum(IN_SPLITS)

kernel_name = 'hybrid_dsa_moba_macaron_adaln'


def rms_norm(x, g):
    xf = x.astype(jnp.float32)
    y = xf * lax.rsqrt(jnp.mean(xf * xf, axis=-1, keepdims=True) + EPS)
    return (y * g.astype(jnp.float32)).astype(x.dtype)


def modulate(h, shift, scale):
    return h * (1.0 + scale) + shift


def partial_rotary(x, pos):
    half = ROPE_DIM // 2
    inv_freq = jnp.power(ROPE_THETA, -jnp.arange(half, dtype=jnp.float32) / half)
    ang = pos.astype(jnp.float32)[..., None] * inv_freq
    if x.ndim == 4:
        ang = ang[:, :, None, :]
    cos, sin = jnp.cos(ang), jnp.sin(ang)
    xr = x[..., :ROPE_DIM].astype(jnp.float32)
    x1, x2 = xr[..., :half], xr[..., half:]
    rot = jnp.concatenate([x1 * cos - x2 * sin, x2 * cos + x1 * sin], axis=-1).astype(x.dtype)
    return jnp.concatenate([rot, x[..., ROPE_DIM:]], axis=-1)


def swiglu(h, w_in, w_out):
    a, b = jnp.split(h @ w_in, 2, axis=-1)
    return (jax.nn.silu(a) * b) @ w_out


def dsa_attention(q_a, ckv, k_rope, q_idx, k_idx, w_idx, w_uk, w_uv):
    B, S = q_a.shape[0], q_a.shape[1]
    top_k = min(IDX_TOPK, S // 4)
    key_pos = jnp.arange(S)
    k_idx32 = k_idx.astype(jnp.float32)
    b_ix = jnp.arange(B)[:, None, None]

    def block(i):
        start = i * Q_BLOCK
        q_pos = start + jnp.arange(Q_BLOCK)
        qi = lax.dynamic_slice_in_dim(q_idx, start, Q_BLOCK, axis=1).astype(jnp.float32)
        wi = lax.dynamic_slice_in_dim(w_idx, start, Q_BLOCK, axis=1).astype(jnp.float32)
        qa = lax.dynamic_slice_in_dim(q_a, start, Q_BLOCK, axis=1)
        logits = jnp.einsum('bqhd,bsd->bqhs', qi, k_idx32) * IDX_DIM ** -0.5
        score = jnp.einsum('bqh,bqhs->bqs', wi * IDX_HEADS ** -0.5, jax.nn.relu(logits))
        causal = key_pos[None, :] <= q_pos[:, None]
        score = jnp.where(causal[None], score, -jnp.inf)
        _, sel = lax.top_k(score, top_k)
        valid = sel <= q_pos[None, :, None]
        c_sel = ckv[b_ix, sel]
        r_sel = k_rope[b_ix, sel]
        q_lat = jnp.einsum('bqhn,rhn->bqhr', qa[..., ROPE_DIM:], w_uk)
        s = (jnp.einsum('bqhr,bqkr->bhqk', q_lat, c_sel)
             + jnp.einsum('bqhe,bqke->bhqk', qa[..., :ROPE_DIM], r_sel)) * HEAD_DIM ** -0.5
        s = jnp.where(valid[:, None], s.astype(jnp.float32), -jnp.inf)
        p = jax.nn.softmax(s, axis=-1).astype(c_sel.dtype)
        o_lat = jnp.einsum('bhqk,bqkr->bqhr', p, c_sel)
        return jnp.einsum('bqhr,rhd->bqhd', o_lat, w_uv)

    out = lax.map(block, jnp.arange(S // Q_BLOCK))
    return out.transpose(1, 0, 2, 3, 4).reshape(B, S, WIDTH_A)


def moba_attention(q, k, v):
    B, S, H, Dh = q.shape
    n_kb = -(-S // MOBA_BLOCK)
    pad = n_kb * MOBA_BLOCK - S

    def to_blocks(t):
        t = jnp.pad(t, ((0, 0), (0, pad), (0, 0), (0, 0)))
        return t.reshape(B, n_kb, MOBA_BLOCK, H, Dh).transpose(0, 3, 1, 2, 4)

    k_blk, v_blk = to_blocks(k), to_blocks(v)
    k_mean = jnp.mean(k_blk.astype(jnp.float32), axis=3).astype(q.dtype)
    n_sel = min(MOBA_TOPK, n_kb - 1)
    q_t = q.transpose(0, 2, 1, 3)
    b_ix = jnp.arange(B)[:, None, None, None]
    h_ix = jnp.arange(H)[None, :, None, None]
    blk_ids = jnp.arange(n_kb)
    scale = Dh ** -0.5

    def block(i):
        start = i * Q_BLOCK
        q_pos = start + jnp.arange(Q_BLOCK)
        own = start // MOBA_BLOCK
        qc = lax.dynamic_slice_in_dim(q_t, start, Q_BLOCK, axis=2)
        k_own = lax.dynamic_index_in_dim(k_blk, own, axis=2, keepdims=False)
        v_own = lax.dynamic_index_in_dim(v_blk, own, axis=2, keepdims=False)
        own_pos = own * MOBA_BLOCK + jnp.arange(MOBA_BLOCK)
        s_own = jnp.einsum('bhqd,bhkd->bhqk', qc, k_own).astype(jnp.float32) * scale
        s_own = jnp.where(own_pos[None, :] <= q_pos[:, None], s_own, -jnp.inf)
        if n_sel == 0:
            p = jax.nn.softmax(s_own, axis=-1).astype(v_own.dtype)
            return jnp.einsum('bhqk,bhkd->bhqd', p, v_own)
        gate = jnp.einsum('bhqd,bhnd->bhqn', qc, k_mean).astype(jnp.float32)
        gate = jnp.where(blk_ids < own, gate, -jnp.inf)
        _, sel = lax.top_k(gate, n_sel)
        sel_valid = sel < own
        k_sel = k_blk[b_ix, h_ix, sel]
        v_sel = v_blk[b_ix, h_ix, sel]
        s_sel = jnp.einsum('bhqd,bhqnkd->bhqnk', qc, k_sel).astype(jnp.float32) * scale
        s_sel = jnp.where(sel_valid[..., None], s_sel, -jnp.inf)
        s = jnp.concatenate([s_sel.reshape(B, H, Q_BLOCK, n_sel * MOBA_BLOCK), s_own], axis=-1)
        p = jax.nn.softmax(s, axis=-1).astype(v_own.dtype)
        p_sel = p[..., :n_sel * MOBA_BLOCK].reshape(B, H, Q_BLOCK, n_sel, MOBA_BLOCK)
        p_own = p[..., n_sel * MOBA_BLOCK:]
        return (jnp.einsum('bhqnk,bhqnkd->bhqd', p_sel, v_sel)
                + jnp.einsum('bhqk,bhkd->bhqd', p_own, v_own))

    out = lax.map(block, jnp.arange(S // Q_BLOCK))
    return out.transpose(1, 0, 3, 2, 4).reshape(B, S, H * Dh)


def hybrid_mixer(h, pos, w_in, kv_norm_g, w_uk, w_uv, w_branch_a, w_branch_b, w_out):
    B, S, _ = h.shape
    proj = h @ w_in
    (q_a, ckv, k_rope, q_idx, k_idx, w_idx, q_b, k_b, v_b, gate_a, gate_b) = jnp.split(
        proj, np.cumsum(IN_SPLITS)[:-1].tolist(), axis=-1)
    q_a = partial_rotary(q_a.reshape(B, S, N_HEADS_A, HEAD_DIM), pos)
    ckv = rms_norm(ckv, kv_norm_g)
    k_rope = partial_rotary(k_rope, pos)
    q_idx = partial_rotary(q_idx.reshape(B, S, IDX_HEADS, IDX_DIM), pos)
    k_idx = partial_rotary(k_idx, pos)
    o_a = dsa_attention(q_a, ckv, k_rope, q_idx, k_idx, w_idx, w_uk, w_uv)
    q_b = partial_rotary(q_b.reshape(B, S, N_HEADS_B, HEAD_DIM), pos)
    k_b = partial_rotary(k_b.reshape(B, S, N_HEADS_B, HEAD_DIM), pos)
    v_b = v_b.reshape(B, S, N_HEADS_B, HEAD_DIM)
    o_b = moba_attention(q_b, k_b, v_b)
    y = jax.nn.sigmoid(gate_a) * (o_a @ w_branch_a) + jax.nn.sigmoid(gate_b) * (o_b @ w_branch_b)
    return y @ w_out


def setup_inputs(seed: int = 0) -> dict:
    key = jax.random.key(seed)
    ks = jax.random.split(key, 20)

    def nrm(k, shape, scale):
        return jax.random.normal(k, shape, jnp.float32) * scale

    return {
        'x': nrm(ks[0], (BATCH, SEQ, D_MODEL), 1.0),
        'c': nrm(ks[1], (BATCH, D_MODEL), 1.0),
        'positions': jnp.tile(jnp.arange(SEQ, dtype=jnp.int32)[None, :], (BATCH, 1)),
        'ada_w': nrm(ks[2], (DEPTH, D_MODEL, N_MOD * D_MODEL), D_MODEL ** -0.5),
        'ada_b': nrm(ks[3], (DEPTH, N_MOD * D_MODEL), 0.02),
        'norm1_g': 1.0 + nrm(ks[4], (DEPTH, D_MODEL), 0.05),
        'ffn1_w_in': nrm(ks[5], (DEPTH, D_MODEL, 2 * D_FF), D_MODEL ** -0.5),
        'ffn1_w_out': nrm(ks[6], (DEPTH, D_FF, D_MODEL), D_FF ** -0.5),
        'norm2_g': 1.0 + nrm(ks[7], (DEPTH, D_MODEL), 0.05),
        'w_in': nrm(ks[8], (DEPTH, D_MODEL, IN_COLS), D_MODEL ** -0.5),
        'kv_norm_g': 1.0 + nrm(ks[9], (DEPTH, KV_RANK), 0.05),
        'w_uk': nrm(ks[10], (DEPTH, KV_RANK, N_HEADS_A, NOPE_DIM), KV_RANK ** -0.5),
        'w_uv': nrm(ks[11], (DEPTH, KV_RANK, N_HEADS_A, HEAD_DIM), KV_RANK ** -0.5),
        'w_branch_a': nrm(ks[12], (DEPTH, WIDTH_A, D_MODEL), WIDTH_A ** -0.5),
        'w_branch_b': nrm(ks[13], (DEPTH, WIDTH_B, D_MODEL), WIDTH_B ** -0.5),
        'w_out': nrm(ks[14], (DEPTH, D_MODEL, D_MODEL), D_MODEL ** -0.5),
        'norm3_g': 1.0 + nrm(ks[15], (DEPTH, D_MODEL), 0.05),
        'ffn2_w_in': nrm(ks[16], (DEPTH, D_MODEL, 2 * D_FF), D_MODEL ** -0.5),
        'ffn2_w_out': nrm(ks[17], (DEPTH, D_FF, D_MODEL), D_FF ** -0.5),
        'final_g': 1.0 + nrm(ks[18], (D_MODEL,), 0.05),
    }


def reference(x, c, positions, ada_w, ada_b, norm1_g, ffn1_w_in, ffn1_w_out, norm2_g, w_in,
              kv_norm_g, w_uk, w_uv, w_branch_a, w_branch_b, w_out, norm3_g, ffn2_w_in,
              ffn2_w_out, final_g):
    B = x.shape[0]
    c_act = jax.nn.silu(c)
    for l in range(DEPTH):
        mod = (c_act @ ada_w[l] + ada_b[l]).reshape(B, N_MOD, 1, D_MODEL)
        sh1, sc1, g1 = mod[:, 0], mod[:, 1], mod[:, 2]
        sh2, sc2, g2 = mod[:, 3], mod[:, 4], mod[:, 5]
        sh3, sc3, g3 = mod[:, 6], mod[:, 7], mod[:, 8]
        h = modulate(rms_norm(x, norm1_g[l]), sh1, sc1)
        x = x + 0.5 * g1 * swiglu(h, ffn1_w_in[l], ffn1_w_out[l])
        h = modulate(rms_norm(x, norm2_g[l]), sh2, sc2)
        x = x + g2 * hybrid_mixer(h, positions, w_in[l], kv_norm_g[l], w_uk[l], w_uv[l],
                                  w_branch_a[l], w_branch_b[l], w_out[l])
        h = modulate(rms_norm(x, norm3_g[l]), sh3, sc3)
        x = x + 0.5 * g3 * swiglu(h, ffn2_w_in[l], ffn2_w_out[l])
    return rms_norm(x, final_g)
```

```python
import functools

import jax
import jax.numpy as jnp
from jax import lax
from jax.experimental import pallas as pl
from jax.experimental.pallas import tpu as pltpu

D_MODEL = 1024
N_HEADS = 8
HEAD_DIM = 64
ROPE_DIM = 16
NOPE_DIM = HEAD_DIM - ROPE_DIM
KV_RANK = 128
IDX_TOPK = 256
MOBA_BLOCK = 256
MOBA_TOPK = 3
Q_BLOCK = 128
D_FF = 2816
ROPE_THETA = 500000.0
EPS = 1e-6
N_MOD = 9
WIDTH = N_HEADS * HEAD_DIM

LANES = 128
KV_TILE = 256
VMEM_LIMIT = 52 * 1024 * 1024

F32 = jnp.float32
BF16 = jnp.bfloat16
INT_MIN = -(2 ** 31)
MASK_BIAS = -2e30
M_INIT = -1e30
NT_DIMS = (((1,), (1,)), ((), ()))


def _dot(a, b):
    return jnp.dot(a, b, preferred_element_type=F32)


def _dot_nt(a, b):
    return lax.dot_general(a, b, NT_DIMS, preferred_element_type=F32)


def _norm_mod(x, g, shift, scale):
    y = x * lax.rsqrt(jnp.mean(x * x, axis=-1, keepdims=True) + EPS)
    return (y * g) * (1.0 + scale) + shift


def _adaln_kernel(c_ref, w_ref, b_ref, o_ref):
    c = c_ref[...]
    act = c * jax.nn.sigmoid(c)
    o_ref[...] = jnp.dot(act, w_ref[...], preferred_element_type=F32,
                         precision=lax.Precision.HIGHEST) + b_ref[...]


def _adaln(c, ada_w, ada_b):
    b = c.shape[0]
    n = ada_w.shape[1]
    rows = 8
    c_pad = jnp.zeros((rows, D_MODEL), F32).at[:b].set(c)
    tn = 1536
    out = pl.pallas_call(
        _adaln_kernel,
        out_shape=jax.ShapeDtypeStruct((rows, n), F32),
        grid=(n // tn,),
        in_specs=[pl.BlockSpec((rows, D_MODEL), lambda j: (0, 0)),
                  pl.BlockSpec((D_MODEL, tn), lambda j: (0, j)),
                  pl.BlockSpec((1, tn), lambda j: (0, j))],
        out_specs=pl.BlockSpec((rows, tn), lambda j: (0, j)),
        compiler_params=pltpu.CompilerParams(dimension_semantics=("arbitrary",),
                                             vmem_limit_bytes=VMEM_LIMIT),
        name="adaln",
    )(c_pad, ada_w, ada_b.reshape(1, n))
    return out[:b].reshape(b, N_MOD, D_MODEL)


FF_CHUNK = D_FF // 2


def _ffn_kernel(x_ref, mod_ref, g_ref, win_ref, wout_ref, fg_ref, o_ref, *, mod_row, final):
    x = x_ref[0]
    mod = mod_ref[0]
    shift, scale, gate = (mod[mod_row + k:mod_row + k + 1] for k in range(3))
    hb = _norm_mod(x, g_ref[...], shift, scale).astype(BF16)
    acc = jnp.zeros(x.shape, F32)
    for j in range(D_FF // FF_CHUNK):
        lo = j * FF_CHUNK
        a = _dot(hb, win_ref[:, lo:lo + FF_CHUNK])
        b = _dot(hb, win_ref[:, D_FF + lo:D_FF + lo + FF_CHUNK])
        act = (a * jax.nn.sigmoid(a) * b).astype(BF16)
        acc = acc + _dot(act, wout_ref[lo:lo + FF_CHUNK, :])
    y = x + (0.5 * gate) * acc
    if final:
        y = (y * lax.rsqrt(jnp.mean(y * y, axis=-1, keepdims=True) + EPS)) * fg_ref[...]
    o_ref[0] = y


def _ffn(x, mod, norm_g, w_in, w_out, final_g, *, mod_row, final, tm=512):
    b, s, d = x.shape
    const = lambda bi, i: (0, 0)
    single = pl.Buffered(1)
    return pl.pallas_call(
        functools.partial(_ffn_kernel, mod_row=mod_row, final=final),
        out_shape=jax.ShapeDtypeStruct(x.shape, F32),
        grid=(b, s // tm),
        in_specs=[pl.BlockSpec((1, tm, d), lambda bi, i: (bi, i, 0)),
                  pl.BlockSpec((1, N_MOD, d), lambda bi, i: (bi, 0, 0)),
                  pl.BlockSpec((1, d), const),
                  pl.BlockSpec((d, 2 * D_FF), const, pipeline_mode=single),
                  pl.BlockSpec((D_FF, d), const, pipeline_mode=single),
                  pl.BlockSpec((1, d), const)],
        out_specs=pl.BlockSpec((1, tm, d), lambda bi, i: (bi, i, 0)),
        compiler_params=pltpu.CompilerParams(dimension_semantics=("arbitrary", "arbitrary"),
                                             vmem_limit_bytes=VMEM_LIMIT),
        name="ffn_final" if final else "ffn",
    )(x, mod, norm_g.reshape(1, d), w_in, w_out, final_g.reshape(1, d))


PROJ_COLS = 5 * WIDTH + 3 * LANES
MISC_W_LANE = ROPE_DIM
ROPE_HALF = ROPE_DIM // 2


def _proj_kernel(x_ref, pos_ref, mod_ref, g_ref, w_ref, kvg_ref, invf_ref,
                 qa_ref, qi_ref, qb_ref, kb_ref, vb_ref, kc_ref, ki_ref, wm_ref):
    x = x_ref[0]
    mod = mod_ref[0]
    hb = _norm_mod(x, g_ref[...], mod[3:4], mod[4:5]).astype(BF16)
    p = _dot(hb, w_ref[...])

    ang = pos_ref[0].astype(F32) * invf_ref[...]
    cos, sin = jnp.cos(ang), jnp.sin(ang)
    lane = lax.broadcasted_iota(jnp.int32, (1, LANES), 1) % HEAD_DIM
    sin_lo = jnp.where(lane < ROPE_HALF, -sin, 0.0)
    sin_hi = jnp.where(lane >= ROPE_HALF, sin, 0.0)

    def rot(xc):
        return (xc * cos + pltpu.roll(xc, LANES - ROPE_HALF, 1) * sin_lo
                + pltpu.roll(xc, ROPE_HALF, 1) * sin_hi)

    def chunk(k):
        return p[:, k * LANES:(k + 1) * LANES]

    n_c = WIDTH // LANES
    for out_ref, base, rotate in ((qa_ref, 0, True), (qi_ref, n_c, True), (qb_ref, 2 * n_c, True),
                                  (kb_ref, 3 * n_c, True), (vb_ref, 4 * n_c, False)):
        for k in range(n_c):
            xc = chunk(base + k)
            out_ref[0, :, k * LANES:(k + 1) * LANES] = (rot(xc) if rotate else xc).astype(BF16)

    ckv = chunk(5 * n_c)
    ckv = (ckv * lax.rsqrt(jnp.mean(ckv * ckv, axis=-1, keepdims=True) + EPS)) * kvg_ref[...]
    misc = rot(chunk(5 * n_c + 1))
    lane_full = lax.broadcasted_iota(jnp.int32, (1, LANES), 1)
    kc_ref[0, :, :LANES] = ckv.astype(BF16)
    kc_ref[0, :, LANES:] = jnp.where(lane_full < ROPE_DIM, misc, 0.0).astype(BF16)
    ki_ref[0] = rot(chunk(5 * n_c + 2)).astype(BF16)
    wm_ref[0] = misc


def _proj(x, pos3, mod, norm_g, w_proj, kv_g, invf, *, tm=512):
    b, s, d = x.shape
    const = lambda bi, i: (0, 0)
    tok = lambda bi, i: (bi, i, 0)
    wide = jax.ShapeDtypeStruct((b, s, WIDTH), BF16)
    return pl.pallas_call(
        _proj_kernel,
        out_shape=(wide, wide, wide, wide, wide,
                   jax.ShapeDtypeStruct((b, s, 2 * LANES), BF16),
                   jax.ShapeDtypeStruct((b, s, LANES), BF16),
                   jax.ShapeDtypeStruct((b, s, LANES), F32)),
        grid=(b, s // tm),
        in_specs=[pl.BlockSpec((1, tm, d), tok),
                  pl.BlockSpec((1, tm, 1), tok),
                  pl.BlockSpec((1, N_MOD, d), lambda bi, i: (bi, 0, 0)),
                  pl.BlockSpec((1, d), const),
                  pl.BlockSpec((d, PROJ_COLS), const),
                  pl.BlockSpec((1, KV_RANK), const),
                  pl.BlockSpec((1, LANES), const)],
        out_specs=tuple(pl.BlockSpec((1, tm, w), tok)
                        for w in (WIDTH,) * 5 + (2 * LANES, LANES, LANES)),
        compiler_params=pltpu.CompilerParams(dimension_semantics=("arbitrary", "arbitrary"),
                                             vmem_limit_bytes=VMEM_LIMIT),
        name="proj",
    )(x, pos3, mod, norm_g.reshape(1, d), w_proj, kv_g.reshape(1, KV_RANK), invf)


def _dsa_kernel(qa_ref, qi_ref, wm_ref, kc_ref, ki_ref, wq_ref, wuv_ref, o_ref,
                key_ref, ql_ref, qe_ref, wb_ref, *, seq):
    i = pl.program_id(1)
    n_tiles = (i * Q_BLOCK + Q_BLOCK + KV_TILE - 1) // KV_TILE
    rows = N_HEADS * Q_BLOCK

    qi = qi_ref[0]
    qa = qa_ref[0]
    ws = (wm_ref[0] * (N_HEADS ** -0.5)) * (HEAD_DIM ** -0.5)
    lane = lax.broadcasted_iota(jnp.int32, (1, LANES), 1)
    for h in range(N_HEADS):
        c, half = divmod(h, 2)
        xc = qi[:, c * LANES:(c + 1) * LANES]
        keep = (lane < HEAD_DIM) if half == 0 else (lane >= HEAD_DIM)
        ql_ref[h * Q_BLOCK:(h + 1) * Q_BLOCK, :] = jnp.where(keep, xc, jnp.zeros_like(xc))
        qe_ref[h * Q_BLOCK:(h + 1) * Q_BLOCK, :] = _dot(qa, wq_ref[h]).astype(BF16)
        wcol = ws[:, MISC_W_LANE + h:MISC_W_LANE + h + 1]
        wb_ref[h] = jnp.broadcast_to(wcol, (Q_BLOCK, KV_TILE))

    q_pos = i * Q_BLOCK + lax.broadcasted_iota(jnp.int32, (Q_BLOCK, KV_TILE), 0)
    k_iota = lax.broadcasted_iota(jnp.int32, (Q_BLOCK, KV_TILE), 1)

    def score_tile(j, carry):
        k0 = pl.multiple_of(j * KV_TILE, KV_TILE)
        logits = _dot_nt(ql_ref[...], ki_ref[0, pl.ds(k0, KV_TILE), :])
        sc = jnp.zeros((Q_BLOCK, KV_TILE), F32)
        for h in range(N_HEADS):
            sc = sc + jnp.maximum(logits[h * Q_BLOCK:(h + 1) * Q_BLOCK], 0.0) * wb_ref[h]
        sc = jnp.where(k0 + k_iota <= q_pos, sc, -jnp.inf)
        bits = pltpu.bitcast(sc, jnp.int32)
        key_ref[:, pl.ds(k0, KV_TILE)] = bits ^ ((bits >> 31) & jnp.int32(0x7FFFFFFF))
        return carry

    lax.fori_loop(0, n_tiles, score_tile, 0)

    def count(pred):
        def body(j, acc):
            k0 = pl.multiple_of(j * KV_TILE, KV_TILE)
            return acc + jnp.where(pred(key_ref[:, pl.ds(k0, KV_TILE)], k0), 1.0, 0.0)
        acc = lax.fori_loop(0, n_tiles, body, jnp.zeros((Q_BLOCK, KV_TILE), F32))
        return jnp.sum(acc, axis=1, keepdims=True)

    def value_bit(b, t_u):
        cand = t_u | jnp.left_shift(jnp.int32(1), 31 - b)
        cand_b = jnp.broadcast_to(cand ^ jnp.int32(INT_MIN), (Q_BLOCK, KV_TILE))
        cnt = count(lambda k, k0: k >= cand_b)
        return jnp.where(cnt >= IDX_TOPK, cand, t_u)

    t_u = lax.fori_loop(0, 32, value_bit, jnp.zeros((Q_BLOCK, 1), jnp.int32))
    thr = jnp.broadcast_to(t_u ^ jnp.int32(INT_MIN), (Q_BLOCK, KV_TILE))

    n_gt = count(lambda k, k0: k > thr)
    n_ge = count(lambda k, k0: k >= thr)
    need = IDX_TOPK - n_gt
    idx_bits = max(1, (seq - 1).bit_length())

    def index_bit(b, cut):
        cand = cut | jnp.left_shift(jnp.int32(1), idx_bits - 1 - b)
        cand_b = jnp.broadcast_to(cand, (Q_BLOCK, KV_TILE))
        cnt = count(lambda k, k0: (k == thr) & (k0 + k_iota < cand_b))
        return jnp.where(cnt < need, cand, cut)

    overflow = jnp.max(jnp.where(n_ge - n_gt > need, 1.0, 0.0)) > 0.0
    cut = lax.cond(overflow,
                   lambda: lax.fori_loop(0, idx_bits, index_bit, jnp.zeros((Q_BLOCK, 1), jnp.int32)),
                   lambda: jnp.full((Q_BLOCK, 1), seq, jnp.int32))
    cut_b = jnp.broadcast_to(cut, (Q_BLOCK, KV_TILE))

    def attend(j, carry):
        m, l, acc = carry
        k0 = pl.multiple_of(j * KV_TILE, KV_TILE)
        kt = kc_ref[0, pl.ds(k0, KV_TILE), :]
        s = _dot_nt(qe_ref[...], kt)
        key = key_ref[:, pl.ds(k0, KV_TILE)]
        k_pos = k0 + k_iota
        sel = (k_pos <= q_pos) & ((key > thr) | ((key == thr) & (k_pos <= cut_b)))
        bias = jnp.where(sel, 0.0, MASK_BIAS)
        s3 = s.reshape(N_HEADS, Q_BLOCK, KV_TILE) + bias[None]
        m_new = jnp.maximum(m, jnp.max(s3, axis=-1, keepdims=True))
        p = jnp.exp(s3 - m_new)
        alpha = jnp.exp(m - m_new)
        l = alpha * l + jnp.sum(p, axis=-1, keepdims=True)
        pv = _dot(p.reshape(rows, KV_TILE).astype(BF16), kt[:, :KV_RANK])
        acc = alpha * acc + pv.reshape(N_HEADS, Q_BLOCK, KV_RANK)
        return m_new, l, acc

    m0 = jnp.full((N_HEADS, Q_BLOCK, 1), M_INIT, F32)
    l0 = jnp.zeros((N_HEADS, Q_BLOCK, 1), F32)
    a0 = jnp.zeros((N_HEADS, Q_BLOCK, KV_RANK), F32)
    _, l, acc = lax.fori_loop(0, n_tiles, attend, (m0, l0, a0))
    o_lat = (acc / l).astype(BF16)
    out = jnp.zeros((Q_BLOCK, WIDTH), F32)
    for h in range(N_HEADS):
        out = out + _dot(o_lat[h], wuv_ref[h])
    o_ref[0] = out.astype(BF16)


def _dsa(qa, qi, wm, kc, ki, wq, wuv):
    b, s, _ = qa.shape
    blk = lambda bi, i: (bi, i, 0)
    res = lambda bi, i: (bi, 0, 0)
    const3 = lambda bi, i: (0, 0, 0)
    rows = N_HEADS * Q_BLOCK
    return pl.pallas_call(
        functools.partial(_dsa_kernel, seq=s),
        out_shape=jax.ShapeDtypeStruct((b, s, WIDTH), BF16),
        grid=(b, s // Q_BLOCK),
        in_specs=[pl.BlockSpec((1, Q_BLOCK, WIDTH), blk),
                  pl.BlockSpec((1, Q_BLOCK, WIDTH), blk),
                  pl.BlockSpec((1, Q_BLOCK, LANES), blk),
                  pl.BlockSpec((1, s, 2 * LANES), res),
                  pl.BlockSpec((1, s, LANES), res),
                  pl.BlockSpec((N_HEADS, WIDTH, 2 * LANES), const3),
                  pl.BlockSpec((N_HEADS, KV_RANK, WIDTH), const3)],
        out_specs=pl.BlockSpec((1, Q_BLOCK, WIDTH), blk),
        scratch_shapes=[pltpu.VMEM((Q_BLOCK, s), jnp.int32),
                        pltpu.VMEM((rows, LANES), BF16),
                        pltpu.VMEM((rows, 2 * LANES), BF16),
                        pltpu.VMEM((N_HEADS, Q_BLOCK, KV_TILE), F32)],
        compiler_params=pltpu.CompilerParams(dimension_semantics=("arbitrary", "arbitrary"),
                                             vmem_limit_bytes=VMEM_LIMIT),
        name="dsa",
    )(qa, qi, wm, kc, ki, wq, wuv)


def _moba_kernel(q_ref, k_ref, v_ref, o_ref, kmean_ref, *, n_blocks):
    i = pl.program_id(2)
    tq = MOBA_BLOCK

    @pl.when(i == 0)
    def _():
        for n in range(n_blocks):
            kb = k_ref[0, n * MOBA_BLOCK:(n + 1) * MOBA_BLOCK, :].astype(F32)
            kmean_ref[n:n + 1, :] = jnp.mean(kb, axis=0, keepdims=True)

    q = q_ref[0]
    lane = lax.broadcasted_iota(jnp.int32, (1, LANES), 1)
    blk = lax.broadcasted_iota(jnp.int32, (tq, n_blocks), 1).astype(F32)
    row = lax.broadcasted_iota(jnp.int32, (tq, KV_TILE), 0)
    col = lax.broadcasted_iota(jnp.int32, (tq, KV_TILE), 1)
    own = i.astype(F32)
    n_sel = min(MOBA_TOPK, n_blocks - 1)

    outs = []
    for half in range(2):
        keep = (lane < HEAD_DIM) if half == 0 else (lane >= HEAD_DIM)
        qh = jnp.where(keep, q, jnp.zeros_like(q))

        gate = lax.dot_general(qh.astype(F32), kmean_ref[...], NT_DIMS, preferred_element_type=F32,
                               precision=lax.Precision.HIGHEST)
        gate = jnp.where(blk < own, gate, -jnp.inf)
        chosen = jnp.zeros((tq, n_blocks), F32)
        for _ in range(n_sel):
            best = jnp.max(gate, axis=1, keepdims=True)
            arg = jnp.min(jnp.where(gate == best, blk, float(n_blocks)), axis=1, keepdims=True)
            hit = blk == arg
            chosen = jnp.where(hit & (best > -jnp.inf), 1.0, chosen)
            gate = jnp.where(hit, -jnp.inf, gate)

        def step(k0, bias, carry):
            m, l, acc = carry
            kt = k_ref[0, pl.ds(k0, KV_TILE), :]
            vt = v_ref[0, pl.ds(k0, KV_TILE), :]
            s = _dot_nt(qh, kt) * (HEAD_DIM ** -0.5) + bias
            m_new = jnp.maximum(m, jnp.max(s, axis=-1, keepdims=True))
            p = jnp.exp(s - m_new)
            alpha = jnp.exp(m - m_new)
            l = alpha * l + jnp.sum(p, axis=-1, keepdims=True)
            acc = alpha * acc + _dot(p.astype(BF16), vt)
            return m_new, l, acc

        def past(n, carry):
            picked = jnp.sum(jnp.where(blk == n.astype(F32), chosen, 0.0), axis=1, keepdims=True)
            bias = jnp.where(picked > 0.0, 0.0, MASK_BIAS)
            return step(pl.multiple_of(n * MOBA_BLOCK, MOBA_BLOCK), bias, carry)

        carry = (jnp.full((tq, 1), M_INIT, F32), jnp.zeros((tq, 1), F32), jnp.zeros((tq, LANES), F32))
        carry = lax.fori_loop(0, i, past, carry)
        causal = jnp.where(col <= row, 0.0, MASK_BIAS)
        _, l, acc = step(pl.multiple_of(i * MOBA_BLOCK, MOBA_BLOCK), causal, carry)
        outs.append(acc / l)

    o_ref[0] = jnp.where(lane < HEAD_DIM, outs[0], outs[1]).astype(BF16)


def _moba(qb, kb, vb):
    b, s, _ = qb.shape
    n_blocks = s // MOBA_BLOCK
    n_c = WIDTH // LANES
    return pl.pallas_call(
        functools.partial(_moba_kernel, n_blocks=n_blocks),
        out_shape=jax.ShapeDtypeStruct((b, s, WIDTH), BF16),
        grid=(b, n_c, n_blocks),
        in_specs=[pl.BlockSpec((1, MOBA_BLOCK, LANES), lambda bi, c, i: (bi, i, c)),
                  pl.BlockSpec((1, s, LANES), lambda bi, c, i: (bi, 0, c)),
                  pl.BlockSpec((1, s, LANES), lambda bi, c, i: (bi, 0, c))],
        out_specs=pl.BlockSpec((1, MOBA_BLOCK, LANES), lambda bi, c, i: (bi, i, c)),
        scratch_shapes=[pltpu.VMEM((n_blocks, LANES), F32)],
        compiler_params=pltpu.CompilerParams(
            dimension_semantics=("arbitrary", "arbitrary", "arbitrary"),
            vmem_limit_bytes=VMEM_LIMIT),
        name="moba",
    )(qb, kb, vb)


def _merge_kernel(x_ref, oa_ref, ob_ref, mod_ref, g_ref, wg_ref, wa_ref, wb_ref, wo_ref, o_ref):
    x = x_ref[0]
    mod = mod_ref[0]
    hb = _norm_mod(x, g_ref[...], mod[3:4], mod[4:5]).astype(BF16)
    gates = _dot(hb, wg_ref[...])
    ya = _dot(oa_ref[0], wa_ref[...])
    yb = _dot(ob_ref[0], wb_ref[...])
    y = jax.nn.sigmoid(gates[:, :D_MODEL]) * ya + jax.nn.sigmoid(gates[:, D_MODEL:]) * yb
    o_ref[0] = x + mod[5:6] * _dot(y.astype(BF16), wo_ref[...])


def _merge(x, oa, ob, mod, norm_g, w_gate, w_a, w_b, w_o, *, tm=512):
    b, s, d = x.shape
    const = lambda bi, i: (0, 0)
    tok = lambda bi, i: (bi, i, 0)
    return pl.pallas_call(
        _merge_kernel,
        out_shape=jax.ShapeDtypeStruct(x.shape, F32),
        grid=(b, s // tm),
        in_specs=[pl.BlockSpec((1, tm, d), tok),
                  pl.BlockSpec((1, tm, WIDTH), tok),
                  pl.BlockSpec((1, tm, WIDTH), tok),
                  pl.BlockSpec((1, N_MOD, d), lambda bi, i: (bi, 0, 0)),
                  pl.BlockSpec((1, d), const),
                  pl.BlockSpec((d, 2 * d), const),
                  pl.BlockSpec((WIDTH, d), const),
                  pl.BlockSpec((WIDTH, d), const),
                  pl.BlockSpec((d, d), const)],
        out_specs=pl.BlockSpec((1, tm, d), tok),
        compiler_params=pltpu.CompilerParams(dimension_semantics=("arbitrary", "arbitrary"),
                                             vmem_limit_bytes=VMEM_LIMIT),
        name="merge",
    )(x, oa, ob, mod, norm_g.reshape(1, d), w_gate, w_a, w_b, w_o)


def _pack_proj_weights(w_in):
    splits = (WIDTH, KV_RANK, ROPE_DIM, WIDTH, HEAD_DIM, N_HEADS, WIDTH, WIDTH, WIDTH, D_MODEL, D_MODEL)
    parts, off = [], 0
    for w in splits:
        parts.append(w_in[:, off:off + w])
        off += w
    q_a, ckv, k_rope, q_idx, k_idx, w_idx, q_b, k_b, v_b, gate_a, gate_b = parts
    pad = jnp.zeros((w_in.shape[0], LANES - ROPE_DIM - N_HEADS), w_in.dtype)
    w_proj = jnp.concatenate([q_a, q_idx, q_b, k_b, v_b, ckv, k_rope, w_idx, pad, k_idx, k_idx], axis=1)
    w_gate = jnp.concatenate([gate_a, gate_b], axis=1)
    return w_proj.astype(BF16), w_gate.astype(BF16)


def _pack_dsa_weights(w_uk, w_uv):
    scale = HEAD_DIM ** -0.5
    wq = jnp.zeros((N_HEADS, WIDTH, 2 * LANES), F32)
    wuv = jnp.zeros((N_HEADS, KV_RANK, WIDTH), F32)
    eye = jnp.eye(ROPE_DIM, dtype=F32)
    for h in range(N_HEADS):
        lo = h * HEAD_DIM
        wq = wq.at[h, lo + ROPE_DIM:lo + HEAD_DIM, :KV_RANK].set(w_uk[:, h, :].T * scale)
        wq = wq.at[h, lo:lo + ROPE_DIM, KV_RANK:KV_RANK + ROPE_DIM].set(eye * scale)
        wuv = wuv.at[h, :, lo:lo + HEAD_DIM].set(w_uv[:, h, :])
    return wq.astype(BF16), wuv.astype(BF16)


def _rope_inv_freq():
    half = ROPE_HALF
    inv_freq = jnp.power(ROPE_THETA, -jnp.arange(half, dtype=F32) / half)
    lane = jnp.arange(LANES)
    row = jnp.where(lane % HEAD_DIM < ROPE_DIM, inv_freq[lane % half], 0.0)
    return row.reshape(1, LANES).astype(F32)


def kernel(x, c, positions, ada_w, ada_b, norm1_g, ffn1_w_in, ffn1_w_out, norm2_g, w_in, kv_norm_g,
           w_uk, w_uv, w_branch_a, w_branch_b, w_out, norm3_g, ffn2_w_in, ffn2_w_out, final_g):
    depth = ada_w.shape[0]
    pos3 = positions[..., None]
    invf = _rope_inv_freq()
    for l in range(depth):
        mod = _adaln(c, ada_w[l], ada_b[l])
        x = _ffn(x, mod, norm1_g[l], ffn1_w_in[l].astype(BF16), ffn1_w_out[l].astype(BF16), final_g,
                 mod_row=0, final=False)
        w_proj, w_gate = _pack_proj_weights(w_in[l])
        wq, wuv = _pack_dsa_weights(w_uk[l], w_uv[l])
        qa, qi, qb, kb, vb, kc, ki, wm = _proj(x, pos3, mod, norm2_g[l], w_proj, kv_norm_g[l], invf)
        oa = _dsa(qa, qi, wm, kc, ki, wq, wuv)
        ob = _moba(qb, kb, vb)
        x = _merge(x, oa, ob, mod, norm2_g[l], w_gate, w_branch_a[l].astype(BF16),
                   w_branch_b[l].astype(BF16), w_out[l].astype(BF16))
        last = l == depth - 1
        x = _ffn(x, mod, norm3_g[l], ffn2_w_in[l].astype(BF16), ffn2_w_out[l].astype(BF16), final_g,
                 mod_row=6, final=last)
    return x
```

```python
import functools

import jax
import jax.numpy as jnp
from jax import lax
from jax.experimental import pallas as pl
from jax.experimental.pallas import tpu as pltpu

D_MODEL = 1024
N_HEADS = 8
HEAD_DIM = 64
ROPE_DIM = 16
NOPE_DIM = HEAD_DIM - ROPE_DIM
KV_RANK = 128
IDX_TOPK = 256
MOBA_BLOCK = 256
MOBA_TOPK = 3
Q_BLOCK = 128
D_FF = 2816
ROPE_THETA = 500000.0
EPS = 1e-6
N_MOD = 9
WIDTH = N_HEADS * HEAD_DIM

LANES = 128
SUBLANES = 8
BF16_ROWS = 16
KV_TILE = 256
VMEM_LIMIT = 52 * 1024 * 1024

F32 = jnp.float32
BF16 = jnp.bfloat16
INT_MIN = -(2 ** 31)
MASK_BIAS = -2e30
M_INIT = -1e30
NT_DIMS = (((1,), (1,)), ((), ()))

CKV_ROWS = KV_RANK + BF16_ROWS
VHEAD_ROWS = HEAD_DIM + BF16_ROWS


def _dot(a, b):
    return jnp.dot(a, b, preferred_element_type=F32)


def _dot_nt(a, b):
    return lax.dot_general(a, b, NT_DIMS, preferred_element_type=F32)


def _norm_mod(x, g, shift, scale):
    y = x * lax.rsqrt(jnp.mean(x * x, axis=-1, keepdims=True) + EPS)
    return (y * g) * (1.0 + scale) + shift


def _tree(op, parts):
    parts = list(parts)
    while len(parts) > 1:
        nxt = [op(parts[k], parts[k + 1]) for k in range(0, len(parts) - 1, 2)]
        if len(parts) % 2:
            nxt.append(parts[-1])
        parts = nxt
    return parts[0]


def _fold_rows(x, op):
    return _tree(op, [x[k:k + SUBLANES] for k in range(0, x.shape[0], SUBLANES)])


def _ones_rows(cols):
    row = lax.broadcasted_iota(jnp.int32, (BF16_ROWS, cols), 0)
    return jnp.where(row == 0, 1.0, 0.0).astype(BF16)


def _adaln_kernel(c_ref, w_ref, b_ref, o_ref):
    c = c_ref[...]
    act = c * jax.nn.sigmoid(c)
    o_ref[...] = jnp.dot(act, w_ref[...], preferred_element_type=F32,
                         precision=lax.Precision.HIGHEST) + b_ref[...]


def _adaln(c, ada_w, ada_b):
    b = c.shape[0]
    n = ada_w.shape[1]
    rows = SUBLANES
    c_pad = jnp.zeros((rows, D_MODEL), F32).at[:b].set(c)
    tn = 1536
    out = pl.pallas_call(
        _adaln_kernel,
        out_shape=jax.ShapeDtypeStruct((rows, n), F32),
        grid=(n // tn,),
        in_specs=[pl.BlockSpec((rows, D_MODEL), lambda j: (0, 0)),
                  pl.BlockSpec((D_MODEL, tn), lambda j: (0, j)),
                  pl.BlockSpec((1, tn), lambda j: (0, j))],
        out_specs=pl.BlockSpec((rows, tn), lambda j: (0, j)),
        compiler_params=pltpu.CompilerParams(dimension_semantics=("arbitrary",),
                                             vmem_limit_bytes=VMEM_LIMIT),
        name="adaln",
    )(c_pad, ada_w, ada_b.reshape(1, n))
    return out[:b].reshape(b, N_MOD, D_MODEL)


FF_CHUNK = D_FF // 2


def _ffn_kernel(x_ref, mod_ref, g_ref, win_ref, wout_ref, fg_ref, o_ref, *, mod_row, final):
    x = x_ref[0]
    mod = mod_ref[0]
    shift, scale, gate = (mod[mod_row + k:mod_row + k + 1] for k in range(3))
    hb = _norm_mod(x, g_ref[...], shift, scale).astype(BF16)
    acc = jnp.zeros(x.shape, F32)
    for j in range(D_FF // FF_CHUNK):
        lo = j * FF_CHUNK
        a = _dot(hb, win_ref[:, lo:lo + FF_CHUNK])
        b = _dot(hb, win_ref[:, D_FF + lo:D_FF + lo + FF_CHUNK])
        act = (a * jax.nn.sigmoid(a) * b).astype(BF16)
        acc = acc + _dot(act, wout_ref[lo:lo + FF_CHUNK, :])
    y = x + (0.5 * gate) * acc
    if final:
        y = (y * lax.rsqrt(jnp.mean(y * y, axis=-1, keepdims=True) + EPS)) * fg_ref[...]
    o_ref[0] = y


def _ffn(x, mod, norm_g, w_in, w_out, final_g, *, mod_row, final, tm=512):
    b, s, d = x.shape
    const = lambda bi, i: (0, 0)
    single = pl.Buffered(1)
    return pl.pallas_call(
        functools.partial(_ffn_kernel, mod_row=mod_row, final=final),
        out_shape=jax.ShapeDtypeStruct(x.shape, F32),
        grid=(b, s // tm),
        in_specs=[pl.BlockSpec((1, tm, d), lambda bi, i: (bi, i, 0)),
                  pl.BlockSpec((1, N_MOD, d), lambda bi, i: (bi, 0, 0)),
                  pl.BlockSpec((1, d), const),
                  pl.BlockSpec((d, 2 * D_FF), const, pipeline_mode=single),
                  pl.BlockSpec((D_FF, d), const, pipeline_mode=single),
                  pl.BlockSpec((1, d), const)],
        out_specs=pl.BlockSpec((1, tm, d), lambda bi, i: (bi, i, 0)),
        compiler_params=pltpu.CompilerParams(dimension_semantics=("arbitrary", "arbitrary"),
                                             vmem_limit_bytes=VMEM_LIMIT),
        name="ffn_final" if final else "ffn",
    )(x, mod, norm_g.reshape(1, d), w_in, w_out, final_g.reshape(1, d))


PROJ_COLS = 5 * WIDTH + 3 * LANES
MISC_W_ROW = ROPE_DIM
ROPE_HALF = ROPE_DIM // 2
N_CHUNKS = WIDTH // LANES


def _proj_kernel(x_ref, pos_ref, mod_ref, g_ref, w_ref, kvg_ref, invf_ref,
                 qa_ref, qi_ref, qb_ref, kb_ref, vt_ref, kc_ref, ct_ref, ki_ref, wt_ref):
    x = x_ref[0]
    tm = x.shape[0]
    mod = mod_ref[0]
    hb = _norm_mod(x, g_ref[...], mod[3:4], mod[4:5]).astype(BF16)
    p = _dot(hb, w_ref[...])

    ang = pos_ref[0].astype(F32) * invf_ref[...]
    cos, sin = jnp.cos(ang), jnp.sin(ang)
    lane = lax.broadcasted_iota(jnp.int32, (1, LANES), 1)
    sin_lo = jnp.where(lane % HEAD_DIM < ROPE_HALF, -sin, 0.0)
    sin_hi = jnp.where(lane % HEAD_DIM >= ROPE_HALF, sin, 0.0)

    def rot(xc):
        return (xc * cos + pltpu.roll(xc, LANES - ROPE_HALF, 1) * sin_lo
                + pltpu.roll(xc, ROPE_HALF, 1) * sin_hi)

    def chunk(k):
        return p[:, k * LANES:(k + 1) * LANES]

    for out_ref, base in ((qa_ref, 0), (qi_ref, N_CHUNKS), (qb_ref, 2 * N_CHUNKS), (kb_ref, 3 * N_CHUNKS)):
        for k in range(N_CHUNKS):
            out_ref[0, :, k * LANES:(k + 1) * LANES] = rot(chunk(base + k)).astype(BF16)

    ones = _ones_rows(KV_TILE)
    for k in range(N_CHUNKS):
        vc = chunk(4 * N_CHUNKS + k)
        for t in range(tm // KV_TILE):
            vt = vc[t * KV_TILE:(t + 1) * KV_TILE].T.astype(BF16)
            for half in range(2):
                r0 = half * VHEAD_ROWS
                vt_ref[0, k, t, r0:r0 + HEAD_DIM, :] = vt[half * HEAD_DIM:(half + 1) * HEAD_DIM]
                vt_ref[0, k, t, r0 + HEAD_DIM:r0 + VHEAD_ROWS, :] = ones

    ckv = chunk(5 * N_CHUNKS)
    ckv = (ckv * lax.rsqrt(jnp.mean(ckv * ckv, axis=-1, keepdims=True) + EPS)) * kvg_ref[...]
    misc = rot(chunk(5 * N_CHUNKS + 1))
    kc_ref[0, :, :LANES] = ckv.astype(BF16)
    kc_ref[0, :, LANES:] = jnp.where(lane < ROPE_DIM, misc, 0.0).astype(BF16)
    for t in range(tm // KV_TILE):
        ct_ref[0, t, :KV_RANK, :] = ckv[t * KV_TILE:(t + 1) * KV_TILE].T.astype(BF16)
        ct_ref[0, t, KV_RANK:, :] = ones
    ki_ref[0] = rot(chunk(5 * N_CHUNKS + 2)).astype(BF16)
    wt_ref[0] = misc.T


def _proj(x, pos3, mod, norm_g, w_proj, kv_g, invf, *, tm=512):
    b, s, d = x.shape
    const = lambda bi, i: (0, 0)
    tok = lambda bi, i: (bi, i, 0)
    wide = jax.ShapeDtypeStruct((b, s, WIDTH), BF16)
    n_kt = s // KV_TILE
    t_kt = tm // KV_TILE
    return pl.pallas_call(
        _proj_kernel,
        out_shape=(wide, wide, wide, wide,
                   jax.ShapeDtypeStruct((b, N_CHUNKS, n_kt, 2 * VHEAD_ROWS, KV_TILE), BF16),
                   jax.ShapeDtypeStruct((b, s, 2 * LANES), BF16),
                   jax.ShapeDtypeStruct((b, n_kt, CKV_ROWS, KV_TILE), BF16),
                   jax.ShapeDtypeStruct((b, s, LANES), BF16),
                   jax.ShapeDtypeStruct((b, LANES, s), F32)),
        grid=(b, s // tm),
        in_specs=[pl.BlockSpec((1, tm, d), tok),
                  pl.BlockSpec((1, tm, 1), tok),
                  pl.BlockSpec((1, N_MOD, d), lambda bi, i: (bi, 0, 0)),
                  pl.BlockSpec((1, d), const),
                  pl.BlockSpec((d, PROJ_COLS), const),
                  pl.BlockSpec((1, KV_RANK), const),
                  pl.BlockSpec((1, LANES), const)],
        out_specs=(pl.BlockSpec((1, tm, WIDTH), tok),) * 4 + (
            pl.BlockSpec((1, N_CHUNKS, t_kt, 2 * VHEAD_ROWS, KV_TILE), lambda bi, i: (bi, 0, i, 0, 0)),
            pl.BlockSpec((1, tm, 2 * LANES), tok),
            pl.BlockSpec((1, t_kt, CKV_ROWS, KV_TILE), lambda bi, i: (bi, i, 0, 0)),
            pl.BlockSpec((1, tm, LANES), tok),
            pl.BlockSpec((1, LANES, tm), lambda bi, i: (bi, 0, i))),
        compiler_params=pltpu.CompilerParams(dimension_semantics=("arbitrary", "arbitrary"),
                                             vmem_limit_bytes=VMEM_LIMIT),
        name="proj",
    )(x, pos3, mod, norm_g.reshape(1, d), w_proj, kv_g.reshape(1, KV_RANK), invf)


def _dsa_kernel(qa_ref, qi_ref, wt_ref, kc_ref, ct_ref, ki_ref, wq_ref, wuv_ref, o_ref,
                key_ref, ql_ref, qe_ref, p_ref, acc_ref, *, seq):
    i = pl.program_id(1)
    n_tiles = (i * Q_BLOCK + Q_BLOCK + KV_TILE - 1) // KV_TILE
    cols = N_HEADS * Q_BLOCK

    qi = qi_ref[0]
    qa = qa_ref[0]
    lane = lax.broadcasted_iota(jnp.int32, (1, LANES), 1)
    for h in range(N_HEADS):
        c, half = divmod(h, 2)
        xc = qi[:, c * LANES:(c + 1) * LANES]
        keep = (lane < HEAD_DIM) if half == 0 else (lane >= HEAD_DIM)
        ql_ref[h * Q_BLOCK:(h + 1) * Q_BLOCK, :] = jnp.where(keep, xc, jnp.zeros_like(xc))
        qe_ref[h * Q_BLOCK:(h + 1) * Q_BLOCK, :] = _dot(qa, wq_ref[h]).astype(BF16)
    ws = (wt_ref[0] * (N_HEADS ** -0.5)) * (HEAD_DIM ** -0.5)

    k_iota = lax.broadcasted_iota(jnp.int32, (KV_TILE, Q_BLOCK), 0)
    q_pos = i * Q_BLOCK + lax.broadcasted_iota(jnp.int32, (KV_TILE, Q_BLOCK), 1)

    def score_tile(j, carry):
        k0 = pl.multiple_of(j * KV_TILE, KV_TILE)
        logits = _dot_nt(ki_ref[0, pl.ds(k0, KV_TILE), :], ql_ref[...])
        sc = _tree(jnp.add, [jnp.maximum(logits[:, h * Q_BLOCK:(h + 1) * Q_BLOCK], 0.0)
                             * ws[MISC_W_ROW + h:MISC_W_ROW + h + 1] for h in range(N_HEADS)])
        sc = jnp.where(k0 + k_iota <= q_pos, sc, -jnp.inf)
        bits = pltpu.bitcast(sc, jnp.int32)
        key_ref[pl.ds(k0, KV_TILE), :] = bits ^ ((bits >> 31) & jnp.int32(0x7FFFFFFF))
        return carry

    lax.fori_loop(0, n_tiles, score_tile, 0)

    def count(pred):
        def body(j, acc):
            k0 = pl.multiple_of(j * KV_TILE, KV_TILE)
            hit = jnp.where(pred(key_ref[pl.ds(k0, KV_TILE), :], k0), 1.0, 0.0)
            return acc + _fold_rows(hit, jnp.add)
        acc = lax.fori_loop(0, n_tiles, body, jnp.zeros((SUBLANES, Q_BLOCK), F32))
        return jnp.sum(acc, axis=0, keepdims=True)

    def value_bit(b, t_u):
        cand = t_u | jnp.left_shift(jnp.int32(1), 31 - b)
        cand_s = cand ^ jnp.int32(INT_MIN)
        cnt = count(lambda k, k0: k >= cand_s)
        return jnp.where(cnt >= IDX_TOPK, cand, t_u)

    t_u = lax.fori_loop(0, 32, value_bit, jnp.zeros((1, Q_BLOCK), jnp.int32))
    thr = t_u ^ jnp.int32(INT_MIN)

    n_gt = count(lambda k, k0: k > thr)
    n_ge = count(lambda k, k0: k >= thr)
    need = IDX_TOPK - n_gt
    idx_bits = max(1, (seq - 1).bit_length())

    def index_bit(b, cut):
        cand = cut | jnp.left_shift(jnp.int32(1), idx_bits - 1 - b)
        cnt = count(lambda k, k0: (k == thr) & (k0 + k_iota < cand))
        return jnp.where(cnt < need, cand, cut)

    overflow = jnp.max(jnp.where(n_ge - n_gt > need, 1.0, 0.0)) > 0.0
    cut = lax.cond(overflow,
                   lambda: lax.fori_loop(0, idx_bits, index_bit, jnp.zeros((1, Q_BLOCK), jnp.int32)),
                   lambda: jnp.full((1, Q_BLOCK), seq, jnp.int32))

    acc_ref[...] = jnp.zeros_like(acc_ref)

    def attend(j, m):
        k0 = pl.multiple_of(j * KV_TILE, KV_TILE)
        s = _dot_nt(kc_ref[0, pl.ds(k0, KV_TILE), :], qe_ref[...])
        key = key_ref[pl.ds(k0, KV_TILE), :]
        k_pos = k0 + k_iota
        sel = (k_pos <= q_pos) & ((key > thr) | ((key == thr) & (k_pos <= cut)))
        bias = jnp.where(sel, 0.0, MASK_BIAS)
        m_new, alpha = [], []
        for h in range(N_HEADS):
            sl = slice(h * Q_BLOCK, (h + 1) * Q_BLOCK)
            sh = s[:, sl] + bias
            m_h = jnp.maximum(m[:, sl], jnp.max(_fold_rows(sh, jnp.maximum), axis=0, keepdims=True))
            p_ref[:, sl] = jnp.exp(sh - m_h).astype(BF16)
            m_new.append(m_h)
            alpha.append(jnp.exp(m[:, sl] - m_h))
        pv = _dot(ct_ref[0, j], p_ref[...])
        acc_ref[...] = jnp.concatenate(alpha, axis=1) * acc_ref[...] + pv
        return jnp.concatenate(m_new, axis=1)

    lax.fori_loop(0, n_tiles, attend, jnp.full((1, cols), M_INIT, F32))
    acc = acc_ref[...]
    o_lat = (acc[:KV_RANK] / acc[KV_RANK:KV_RANK + 1]).astype(BF16)
    out_t = jnp.concatenate([_dot(wuv_ref[h], o_lat[:, h * Q_BLOCK:(h + 1) * Q_BLOCK])
                             for h in range(N_HEADS)], axis=0)
    o_ref[0] = out_t.T.astype(BF16)


def _dsa(qa, qi, wt, kc, ct, ki, wq, wuv):
    b, s, _ = qa.shape
    blk = lambda bi, i: (bi, i, 0)
    res = lambda bi, i: (bi, 0, 0)
    const3 = lambda bi, i: (0, 0, 0)
    cols = N_HEADS * Q_BLOCK
    return pl.pallas_call(
        functools.partial(_dsa_kernel, seq=s),
        out_shape=jax.ShapeDtypeStruct((b, s, WIDTH), BF16),
        grid=(b, s // Q_BLOCK),
        in_specs=[pl.BlockSpec((1, Q_BLOCK, WIDTH), blk),
                  pl.BlockSpec((1, Q_BLOCK, WIDTH), blk),
                  pl.BlockSpec((1, LANES, Q_BLOCK), lambda bi, i: (bi, 0, i)),
                  pl.BlockSpec((1, s, 2 * LANES), res),
                  pl.BlockSpec((1, s // KV_TILE, CKV_ROWS, KV_TILE), lambda bi, i: (bi, 0, 0, 0)),
                  pl.BlockSpec((1, s, LANES), res),
                  pl.BlockSpec((N_HEADS, WIDTH, 2 * LANES), const3),
                  pl.BlockSpec((N_HEADS, HEAD_DIM, KV_RANK), const3)],
        out_specs=pl.BlockSpec((1, Q_BLOCK, WIDTH), blk),
        scratch_shapes=[pltpu.VMEM((s, Q_BLOCK), jnp.int32),
                        pltpu.VMEM((cols, LANES), BF16),
                        pltpu.VMEM((cols, 2 * LANES), BF16),
                        pltpu.VMEM((KV_TILE, cols), BF16),
                        pltpu.VMEM((CKV_ROWS, cols), F32)],
        compiler_params=pltpu.CompilerParams(dimension_semantics=("arbitrary", "arbitrary"),
                                             vmem_limit_bytes=VMEM_LIMIT),
        name="dsa",
    )(qa, qi, wt, kc, ct, ki, wq, wuv)


def _moba_kernel(q_ref, k_ref, vt_ref, o_ref, kmean_ref, chosen_ref, *, n_blocks):
    i = pl.program_id(2)
    tq = MOBA_BLOCK

    @pl.when(i == 0)
    def _():
        for n in range(n_blocks):
            kb = k_ref[0, n * MOBA_BLOCK:(n + 1) * MOBA_BLOCK, :].astype(F32)
            kmean_ref[n:n + 1, :] = jnp.mean(kb, axis=0, keepdims=True)

    q = q_ref[0]
    lane = lax.broadcasted_iota(jnp.int32, (1, LANES), 1)
    blk = lax.broadcasted_iota(jnp.int32, (n_blocks, tq), 0).astype(F32)
    own = i.astype(F32)
    n_sel = min(MOBA_TOPK, n_blocks - 1)

    qs = []
    for half in range(2):
        keep = (lane < HEAD_DIM) if half == 0 else (lane >= HEAD_DIM)
        qh = jnp.where(keep, q, jnp.zeros_like(q))
        gate = lax.dot_general(kmean_ref[...], qh.astype(F32), NT_DIMS, preferred_element_type=F32,
                               precision=lax.Precision.HIGHEST)
        gate = jnp.where(blk < own, gate, -jnp.inf)
        chosen = jnp.zeros((n_blocks, tq), F32)
        for _ in range(n_sel):
            best = jnp.max(gate, axis=0, keepdims=True)
            arg = jnp.min(jnp.where(gate == best, blk, float(n_blocks)), axis=0, keepdims=True)
            hit = blk == arg
            chosen = jnp.where(hit & (best > -jnp.inf), 1.0, chosen)
            gate = jnp.where(hit, -jnp.inf, gate)
        chosen_ref[half] = chosen
        qs.append(qh * (HEAD_DIM ** -0.5))

    def step(n, biases, carry):
        kt = k_ref[0, pl.ds(pl.multiple_of(n * MOBA_BLOCK, MOBA_BLOCK), MOBA_BLOCK), :]
        out = []
        for half in range(2):
            m, acc = carry[half]
            s = _dot_nt(kt, qs[half]) + biases[half]
            m_new = jnp.maximum(m, jnp.max(_fold_rows(s, jnp.maximum), axis=0, keepdims=True))
            p = jnp.exp(s - m_new).astype(BF16)
            vt = vt_ref[0, 0, n, half * VHEAD_ROWS:(half + 1) * VHEAD_ROWS, :]
            acc = jnp.exp(m - m_new) * acc + _dot(vt, p)
            out.append((m_new, acc))
        return tuple(out)

    def past(n, carry):
        biases = [jnp.where(chosen_ref[half, pl.ds(n, 1), :] > 0.0, 0.0, MASK_BIAS) for half in range(2)]
        return step(n, biases, carry)

    init = tuple((jnp.full((1, tq), M_INIT, F32), jnp.zeros((VHEAD_ROWS, tq), F32)) for _ in range(2))
    carry = lax.fori_loop(0, i, past, init)
    row = lax.broadcasted_iota(jnp.int32, (KV_TILE, tq), 0)
    col = lax.broadcasted_iota(jnp.int32, (KV_TILE, tq), 1)
    causal = jnp.where(row <= col, 0.0, MASK_BIAS)
    carry = step(i, [causal, causal], carry)
    out_t = jnp.concatenate([acc[:HEAD_DIM] / acc[HEAD_DIM:HEAD_DIM + 1] for _, acc in carry], axis=0)
    o_ref[0] = out_t.T.astype(BF16)


def _moba(qb, kb, vt):
    b, s, _ = qb.shape
    n_blocks = s // MOBA_BLOCK
    return pl.pallas_call(
        functools.partial(_moba_kernel, n_blocks=n_blocks),
        out_shape=jax.ShapeDtypeStruct((b, s, WIDTH), BF16),
        grid=(b, N_CHUNKS, n_blocks),
        in_specs=[pl.BlockSpec((1, MOBA_BLOCK, LANES), lambda bi, c, i: (bi, i, c)),
                  pl.BlockSpec((1, s, LANES), lambda bi, c, i: (bi, 0, c)),
                  pl.BlockSpec((1, 1, n_blocks, 2 * VHEAD_ROWS, KV_TILE), lambda bi, c, i: (bi, c, 0, 0, 0))],
        out_specs=pl.BlockSpec((1, MOBA_BLOCK, LANES), lambda bi, c, i: (bi, i, c)),
        scratch_shapes=[pltpu.VMEM((n_blocks, LANES), F32),
                        pltpu.VMEM((2, n_blocks, MOBA_BLOCK), F32)],
        compiler_params=pltpu.CompilerParams(
            dimension_semantics=("arbitrary", "arbitrary", "arbitrary"),
            vmem_limit_bytes=VMEM_LIMIT),
        name="moba",
    )(qb, kb, vt)


def _merge_kernel(x_ref, oa_ref, ob_ref, mod_ref, g_ref, wg_ref, wa_ref, wb_ref, wo_ref, o_ref):
    x = x_ref[0]
    mod = mod_ref[0]
    hb = _norm_mod(x, g_ref[...], mod[3:4], mod[4:5]).astype(BF16)
    gates = _dot(hb, wg_ref[...])
    ya = _dot(oa_ref[0], wa_ref[...])
    yb = _dot(ob_ref[0], wb_ref[...])
    y = jax.nn.sigmoid(gates[:, :D_MODEL]) * ya + jax.nn.sigmoid(gates[:, D_MODEL:]) * yb
    o_ref[0] = x + mod[5:6] * _dot(y.astype(BF16), wo_ref[...])


def _merge(x, oa, ob, mod, norm_g, w_gate, w_a, w_b, w_o, *, tm=512):
    b, s, d = x.shape
    const = lambda bi, i: (0, 0)
    tok = lambda bi, i: (bi, i, 0)
    return pl.pallas_call(
        _merge_kernel,
        out_shape=jax.ShapeDtypeStruct(x.shape, F32),
        grid=(b, s // tm),
        in_specs=[pl.BlockSpec((1, tm, d), tok),
                  pl.BlockSpec((1, tm, WIDTH), tok),
                  pl.BlockSpec((1, tm, WIDTH), tok),
                  pl.BlockSpec((1, N_MOD, d), lambda bi, i: (bi, 0, 0)),
                  pl.BlockSpec((1, d), const),
                  pl.BlockSpec((d, 2 * d), const),
                  pl.BlockSpec((WIDTH, d), const),
                  pl.BlockSpec((WIDTH, d), const),
                  pl.BlockSpec((d, d), const)],
        out_specs=pl.BlockSpec((1, tm, d), tok),
        compiler_params=pltpu.CompilerParams(dimension_semantics=("arbitrary", "arbitrary"),
                                             vmem_limit_bytes=VMEM_LIMIT),
        name="merge",
    )(x, oa, ob, mod, norm_g.reshape(1, d), w_gate, w_a, w_b, w_o)


def _pack_proj_weights(w_in):
    splits = (WIDTH, KV_RANK, ROPE_DIM, WIDTH, HEAD_DIM, N_HEADS, WIDTH, WIDTH, WIDTH, D_MODEL, D_MODEL)
    parts, off = [], 0
    for w in splits:
        parts.append(w_in[:, off:off + w])
        off += w
    q_a, ckv, k_rope, q_idx, k_idx, w_idx, q_b, k_b, v_b, gate_a, gate_b = parts
    pad = jnp.zeros((w_in.shape[0], LANES - ROPE_DIM - N_HEADS), w_in.dtype)
    w_proj = jnp.concatenate([q_a, q_idx, q_b, k_b, v_b, ckv, k_rope, w_idx, pad, k_idx, k_idx], axis=1)
    w_gate = jnp.concatenate([gate_a, gate_b], axis=1)
    return w_proj.astype(BF16), w_gate.astype(BF16)


def _pack_dsa_weights(w_uk, w_uv):
    scale = HEAD_DIM ** -0.5
    wq = jnp.zeros((N_HEADS, WIDTH, 2 * LANES), F32)
    eye = jnp.eye(ROPE_DIM, dtype=F32)
    for h in range(N_HEADS):
        lo = h * HEAD_DIM
        wq = wq.at[h, lo + ROPE_DIM:lo + HEAD_DIM, :KV_RANK].set(w_uk[:, h, :].T * scale)
        wq = wq.at[h, lo:lo + ROPE_DIM, KV_RANK:KV_RANK + ROPE_DIM].set(eye * scale)
    wuv = jnp.transpose(w_uv, (1, 2, 0))
    return wq.astype(BF16), wuv.astype(BF16)


def _rope_inv_freq():
    half = ROPE_HALF
    inv_freq = jnp.power(ROPE_THETA, -jnp.arange(half, dtype=F32) / half)
    lane = jnp.arange(LANES)
    row = jnp.where(lane % HEAD_DIM < ROPE_DIM, inv_freq[lane % half], 0.0)
    return row.reshape(1, LANES).astype(F32)


def kernel(x, c, positions, ada_w, ada_b, norm1_g, ffn1_w_in, ffn1_w_out, norm2_g, w_in, kv_norm_g,
           w_uk, w_uv, w_branch_a, w_branch_b, w_out, norm3_g, ffn2_w_in, ffn2_w_out, final_g):
    depth = ada_w.shape[0]
    pos3 = positions[..., None]
    invf = _rope_inv_freq()
    for l in range(depth):
        mod = _adaln(c, ada_w[l], ada_b[l])
        x = _ffn(x, mod, norm1_g[l], ffn1_w_in[l].astype(BF16), ffn1_w_out[l].astype(BF16), final_g,
                 mod_row=0, final=False)
        w_proj, w_gate = _pack_proj_weights(w_in[l])
        wq, wuv = _pack_dsa_weights(w_uk[l], w_uv[l])
        qa, qi, qb, kb, vt, kc, ct, ki, wt = _proj(x, pos3, mod, norm2_g[l], w_proj, kv_norm_g[l], invf)
        oa = _dsa(qa, qi, wt, kc, ct, ki, wq, wuv)
        ob = _moba(qb, kb, vt)
        x = _merge(x, oa, ob, mod, norm2_g[l], w_gate, w_branch_a[l].astype(BF16),
                   w_branch_b[l].astype(BF16), w_out[l].astype(BF16))
        last = l == depth - 1
        x = _ffn(x, mod, norm3_g[l], ffn2_w_in[l].astype(BF16), ffn2_w_out[l].astype(BF16), final_g,
                 mod_row=6, final=last)
    return x
```

```python
import functools

import jax
import jax.numpy as jnp
from jax import lax
from jax.experimental import pallas as pl
from jax.experimental.pallas import tpu as pltpu

D_MODEL = 1024
N_HEADS = 8
HEAD_DIM = 64
ROPE_DIM = 16
NOPE_DIM = HEAD_DIM - ROPE_DIM
KV_RANK = 128
IDX_TOPK = 256
MOBA_BLOCK = 256
MOBA_TOPK = 3
Q_BLOCK = 128
D_FF = 2816
ROPE_THETA = 500000.0
EPS = 1e-6
N_MOD = 9
WIDTH = N_HEADS * HEAD_DIM

LANES = 128
SUBLANES = 8
BF16_ROWS = 16
KV_TILE = 256
VMEM_LIMIT = 52 * 1024 * 1024

F32 = jnp.float32
BF16 = jnp.bfloat16
INT_MIN = -(2 ** 31)
HALF16 = 2 ** 15
MASK_BIAS = -2e30
M_INIT = -1e30
NT_DIMS = (((1,), (1,)), ((), ()))

CKV_ROWS = KV_RANK + BF16_ROWS
VHEAD_ROWS = HEAD_DIM + BF16_ROWS

DSA_STEP = 2 * KV_TILE
MOBA_GROUP = 4


def _dot(a, b):
    return jnp.dot(a, b, preferred_element_type=F32)


def _dot_nt(a, b):
    return lax.dot_general(a, b, NT_DIMS, preferred_element_type=F32)


def _norm_mod(x, g, shift, scale):
    y = x * lax.rsqrt(jnp.mean(x * x, axis=-1, keepdims=True) + EPS)
    return (y * g) * (1.0 + scale) + shift


def _tree(op, parts):
    parts = list(parts)
    while len(parts) > 1:
        nxt = [op(parts[k], parts[k + 1]) for k in range(0, len(parts) - 1, 2)]
        if len(parts) % 2:
            nxt.append(parts[-1])
        parts = nxt
    return parts[0]


def _fold_rows(x, op):
    return _tree(op, [x[k:k + SUBLANES] for k in range(0, x.shape[0], SUBLANES)])


def _ones_rows(cols):
    row = lax.broadcasted_iota(jnp.int32, (BF16_ROWS, cols), 0)
    return jnp.where(row == 0, 1.0, 0.0).astype(BF16)


def _adaln_kernel(c_ref, w_ref, b_ref, o_ref):
    c = c_ref[...]
    act = c * jax.nn.sigmoid(c)
    o_ref[...] = jnp.dot(act, w_ref[...], preferred_element_type=F32,
                         precision=lax.Precision.HIGHEST) + b_ref[...]


def _adaln(c, ada_w, ada_b):
    b = c.shape[0]
    n = ada_w.shape[1]
    rows = SUBLANES
    c_pad = jnp.zeros((rows, D_MODEL), F32).at[:b].set(c)
    tn = 1536
    out = pl.pallas_call(
        _adaln_kernel,
        out_shape=jax.ShapeDtypeStruct((rows, n), F32),
        grid=(n // tn,),
        in_specs=[pl.BlockSpec((rows, D_MODEL), lambda j: (0, 0)),
                  pl.BlockSpec((D_MODEL, tn), lambda j: (0, j)),
                  pl.BlockSpec((1, tn), lambda j: (0, j))],
        out_specs=pl.BlockSpec((rows, tn), lambda j: (0, j)),
        compiler_params=pltpu.CompilerParams(dimension_semantics=("arbitrary",),
                                             vmem_limit_bytes=VMEM_LIMIT),
        name="adaln",
    )(c_pad, ada_w, ada_b.reshape(1, n))
    return out[:b].reshape(b, N_MOD, D_MODEL)


FF_CHUNK = D_FF // 2


def _ffn_kernel(x_ref, mod_ref, g_ref, win_ref, wout_ref, fg_ref, o_ref, *, mod_row, final):
    x = x_ref[0]
    mod = mod_ref[0]
    shift, scale, gate = (mod[mod_row + k:mod_row + k + 1] for k in range(3))
    hb = _norm_mod(x, g_ref[...], shift, scale).astype(BF16)
    acc = jnp.zeros(x.shape, F32)
    for j in range(D_FF // FF_CHUNK):
        lo = j * FF_CHUNK
        a = _dot(hb, win_ref[:, lo:lo + FF_CHUNK])
        b = _dot(hb, win_ref[:, D_FF + lo:D_FF + lo + FF_CHUNK])
        act = (a * jax.nn.sigmoid(a) * b).astype(BF16)
        acc = acc + _dot(act, wout_ref[lo:lo + FF_CHUNK, :])
    y = x + (0.5 * gate) * acc
    if final:
        y = (y * lax.rsqrt(jnp.mean(y * y, axis=-1, keepdims=True) + EPS)) * fg_ref[...]
    o_ref[0] = y


def _ffn(x, mod, norm_g, w_in, w_out, final_g, *, mod_row, final, tm=512):
    b, s, d = x.shape
    const = lambda bi, i: (0, 0)
    single = pl.Buffered(1)
    return pl.pallas_call(
        functools.partial(_ffn_kernel, mod_row=mod_row, final=final),
        out_shape=jax.ShapeDtypeStruct(x.shape, F32),
        grid=(b, s // tm),
        in_specs=[pl.BlockSpec((1, tm, d), lambda bi, i: (bi, i, 0)),
                  pl.BlockSpec((1, N_MOD, d), lambda bi, i: (bi, 0, 0)),
                  pl.BlockSpec((1, d), const),
                  pl.BlockSpec((d, 2 * D_FF), const, pipeline_mode=single),
                  pl.BlockSpec((D_FF, d), const, pipeline_mode=single),
                  pl.BlockSpec((1, d), const)],
        out_specs=pl.BlockSpec((1, tm, d), lambda bi, i: (bi, i, 0)),
        compiler_params=pltpu.CompilerParams(dimension_semantics=("arbitrary", "arbitrary"),
                                             vmem_limit_bytes=VMEM_LIMIT),
        name="ffn_final" if final else "ffn",
    )(x, mod, norm_g.reshape(1, d), w_in, w_out, final_g.reshape(1, d))


PROJ_COLS = 5 * WIDTH + 3 * LANES
MISC_W_ROW = ROPE_DIM
ROPE_HALF = ROPE_DIM // 2
N_CHUNKS = WIDTH // LANES


def _proj_kernel(x_ref, pos_ref, mod_ref, g_ref, w_ref, kvg_ref, invf_ref,
                 qa_ref, qi_ref, qb_ref, kb_ref, vt_ref, kc_ref, ct_ref, ki_ref, wt_ref):
    x = x_ref[0]
    tm = x.shape[0]
    mod = mod_ref[0]
    hb = _norm_mod(x, g_ref[...], mod[3:4], mod[4:5]).astype(BF16)
    p = _dot(hb, w_ref[...])

    ang = pos_ref[0].astype(F32) * invf_ref[...]
    cos, sin = jnp.cos(ang), jnp.sin(ang)
    lane = lax.broadcasted_iota(jnp.int32, (1, LANES), 1)
    sin_lo = jnp.where(lane % HEAD_DIM < ROPE_HALF, -sin, 0.0)
    sin_hi = jnp.where(lane % HEAD_DIM >= ROPE_HALF, sin, 0.0)

    def rot(xc):
        return (xc * cos + pltpu.roll(xc, LANES - ROPE_HALF, 1) * sin_lo
                + pltpu.roll(xc, ROPE_HALF, 1) * sin_hi)

    def chunk(k):
        return p[:, k * LANES:(k + 1) * LANES]

    for out_ref, base in ((qa_ref, 0), (qi_ref, N_CHUNKS), (qb_ref, 2 * N_CHUNKS), (kb_ref, 3 * N_CHUNKS)):
        for k in range(N_CHUNKS):
            out_ref[0, :, k * LANES:(k + 1) * LANES] = rot(chunk(base + k)).astype(BF16)

    ones = _ones_rows(KV_TILE)
    for k in range(N_CHUNKS):
        vc = chunk(4 * N_CHUNKS + k)
        for t in range(tm // KV_TILE):
            vt = vc[t * KV_TILE:(t + 1) * KV_TILE].T.astype(BF16)
            for half in range(2):
                r0 = half * VHEAD_ROWS
                vt_ref[0, k, t, r0:r0 + HEAD_DIM, :] = vt[half * HEAD_DIM:(half + 1) * HEAD_DIM]
                vt_ref[0, k, t, r0 + HEAD_DIM:r0 + VHEAD_ROWS, :] = ones

    ckv = chunk(5 * N_CHUNKS)
    ckv = (ckv * lax.rsqrt(jnp.mean(ckv * ckv, axis=-1, keepdims=True) + EPS)) * kvg_ref[...]
    misc = rot(chunk(5 * N_CHUNKS + 1))
    kc_ref[0, :, :LANES] = ckv.astype(BF16)
    kc_ref[0, :, LANES:] = jnp.where(lane < ROPE_DIM, misc, 0.0).astype(BF16)
    for t in range(tm // DSA_STEP):
        ct_ref[0, t, :KV_RANK, :] = ckv[t * DSA_STEP:(t + 1) * DSA_STEP].T.astype(BF16)
        ct_ref[0, t, KV_RANK:, :] = _ones_rows(DSA_STEP)
    ki_ref[0] = rot(chunk(5 * N_CHUNKS + 2)).astype(BF16)
    wt_ref[0] = misc.T


def _proj(x, pos3, mod, norm_g, w_proj, kv_g, invf, *, tm=512):
    b, s, d = x.shape
    const = lambda bi, i: (0, 0)
    tok = lambda bi, i: (bi, i, 0)
    wide = jax.ShapeDtypeStruct((b, s, WIDTH), BF16)
    n_kt = s // KV_TILE
    t_kt = tm // KV_TILE
    return pl.pallas_call(
        _proj_kernel,
        out_shape=(wide, wide, wide, wide,
                   jax.ShapeDtypeStruct((b, N_CHUNKS, n_kt, 2 * VHEAD_ROWS, KV_TILE), BF16),
                   jax.ShapeDtypeStruct((b, s, 2 * LANES), BF16),
                   jax.ShapeDtypeStruct((b, s // DSA_STEP, CKV_ROWS, DSA_STEP), BF16),
                   jax.ShapeDtypeStruct((b, s, LANES), BF16),
                   jax.ShapeDtypeStruct((b, LANES, s), F32)),
        grid=(b, s // tm),
        in_specs=[pl.BlockSpec((1, tm, d), tok),
                  pl.BlockSpec((1, tm, 1), tok),
                  pl.BlockSpec((1, N_MOD, d), lambda bi, i: (bi, 0, 0)),
                  pl.BlockSpec((1, d), const),
                  pl.BlockSpec((d, PROJ_COLS), const),
                  pl.BlockSpec((1, KV_RANK), const),
                  pl.BlockSpec((1, LANES), const)],
        out_specs=(pl.BlockSpec((1, tm, WIDTH), tok),) * 4 + (
            pl.BlockSpec((1, N_CHUNKS, t_kt, 2 * VHEAD_ROWS, KV_TILE), lambda bi, i: (bi, 0, i, 0, 0)),
            pl.BlockSpec((1, tm, 2 * LANES), tok),
            pl.BlockSpec((1, tm // DSA_STEP, CKV_ROWS, DSA_STEP), lambda bi, i: (bi, i, 0, 0)),
            pl.BlockSpec((1, tm, LANES), tok),
            pl.BlockSpec((1, LANES, tm), lambda bi, i: (bi, 0, i))),
        compiler_params=pltpu.CompilerParams(dimension_semantics=("arbitrary", "arbitrary"),
                                             vmem_limit_bytes=VMEM_LIMIT),
        name="proj",
    )(x, pos3, mod, norm_g.reshape(1, d), w_proj, kv_g.reshape(1, KV_RANK), invf)


def _dsa_kernel(qa_ref, qi_ref, wt_ref, kc_ref, ct_ref, ki_ref, wq_ref, wuv_ref, o_ref,
                key_ref, hi_ref, lo_ref, ql_ref, qe_ref, p_ref, acc_ref, *, seq):
    i = pl.program_id(1)
    n_steps = (i * Q_BLOCK + Q_BLOCK + DSA_STEP - 1) // DSA_STEP
    sub = DSA_STEP // KV_TILE
    cols = N_HEADS * Q_BLOCK

    qi = qi_ref[0]
    qa = qa_ref[0]
    lane = lax.broadcasted_iota(jnp.int32, (1, LANES), 1)
    for h in range(N_HEADS):
        c, half = divmod(h, 2)
        xc = qi[:, c * LANES:(c + 1) * LANES]
        keep = (lane < HEAD_DIM) if half == 0 else (lane >= HEAD_DIM)
        ql_ref[h * Q_BLOCK:(h + 1) * Q_BLOCK, :] = jnp.where(keep, xc, jnp.zeros_like(xc))
        qe_ref[h * Q_BLOCK:(h + 1) * Q_BLOCK, :] = _dot(qa, wq_ref[h]).astype(BF16)
    ws = (wt_ref[0] * (N_HEADS ** -0.5)) * (HEAD_DIM ** -0.5)

    k_iota = lax.broadcasted_iota(jnp.int32, (KV_TILE, Q_BLOCK), 0)
    q_pos = i * Q_BLOCK + lax.broadcasted_iota(jnp.int32, (KV_TILE, Q_BLOCK), 1)

    def tiles(step):
        return [pl.multiple_of(step * DSA_STEP + t * KV_TILE, KV_TILE) for t in range(sub)]

    def score_step(step, carry):
        for k0 in tiles(step):
            logits = _dot_nt(ki_ref[0, pl.ds(k0, KV_TILE), :], ql_ref[...])
            sc = _tree(jnp.add, [jnp.maximum(logits[:, h * Q_BLOCK:(h + 1) * Q_BLOCK], 0.0)
                                 * ws[MISC_W_ROW + h:MISC_W_ROW + h + 1] for h in range(N_HEADS)])
            sc = jnp.where(k0 + k_iota <= q_pos, sc, -jnp.inf)
            bits = pltpu.bitcast(sc, jnp.int32)
            key = bits ^ ((bits >> 31) & jnp.int32(0x7FFFFFFF))
            key_ref[pl.ds(k0, KV_TILE), :] = key
            hi_ref[pl.ds(k0, KV_TILE), :] = (key >> 16).astype(jnp.int16)
            lo_ref[pl.ds(k0, KV_TILE), :] = ((key & jnp.int32(0xFFFF)) - HALF16).astype(jnp.int16)
        return carry

    lax.fori_loop(0, n_steps, score_step, 0)

    def count(pred):
        def body(step, acc):
            hits = [_fold_rows(jnp.where(pred(key_ref[pl.ds(k0, KV_TILE), :], k0), 1.0, 0.0), jnp.add)
                    for k0 in tiles(step)]
            return acc + _tree(jnp.add, hits)
        acc = lax.fori_loop(0, n_steps, body, jnp.zeros((SUBLANES, Q_BLOCK), F32))
        return jnp.sum(acc, axis=0, keepdims=True)

    one16 = jnp.ones((BF16_ROWS, Q_BLOCK), jnp.int16)
    zero16 = jnp.zeros((BF16_ROWS, Q_BLOCK), jnp.int16)

    def rows16(v):
        return jnp.broadcast_to(v, (BF16_ROWS, Q_BLOCK)).astype(jnp.int16)

    def count16(ref, pred):
        def body(step, acc):
            hits = []
            for k0 in tiles(step):
                t = ref[pl.ds(k0, KV_TILE), :]
                hits += [jnp.where(pred(t[r:r + BF16_ROWS]), one16, zero16)
                         for r in range(0, KV_TILE, BF16_ROWS)]
            return acc + _tree(jnp.add, hits)
        acc = lax.fori_loop(0, n_steps, body, zero16)
        return jnp.sum(acc.astype(F32), axis=0, keepdims=True)

    def half_bisect(ref, target):
        def bit(b, t_u):
            cand = t_u | jnp.left_shift(jnp.int32(1), 15 - b)
            c16 = rows16(cand - HALF16)
            cnt = count16(ref, lambda t: t >= c16)
            return jnp.where(cnt >= target, cand, t_u)
        return lax.fori_loop(0, 16, bit, jnp.zeros((1, Q_BLOCK), jnp.int32))

    hi_s = half_bisect(hi_ref, float(IDX_TOPK)) - HALF16
    hi16 = rows16(hi_s)
    slots = IDX_TOPK - count16(hi_ref, lambda t: t > hi16)

    hi_tile = jnp.broadcast_to(hi_s, (KV_TILE, Q_BLOCK)).astype(jnp.int16)
    low_min = jnp.full((KV_TILE, Q_BLOCK), -HALF16, jnp.int16)

    def mask_low(step, carry):
        for k0 in tiles(step):
            sl = pl.ds(k0, KV_TILE)
            lo_ref[sl, :] = jnp.where(hi_ref[sl, :] == hi_tile, lo_ref[sl, :], low_min)
        return carry

    lax.fori_loop(0, n_steps, mask_low, 0)
    lo_u = half_bisect(lo_ref, slots)
    thr = hi_s * (2 * HALF16) + lo_u

    n_ge = count(lambda k, k0: k >= thr)
    idx_bits = max(1, (seq - 1).bit_length())

    def tie_cut():
        need = IDX_TOPK - count(lambda k, k0: k > thr)

        def index_bit(b, cut):
            cand = cut | jnp.left_shift(jnp.int32(1), idx_bits - 1 - b)
            cnt = count(lambda k, k0: (k == thr) & (k0 + k_iota < cand))
            return jnp.where(cnt < need, cand, cut)

        return lax.fori_loop(0, idx_bits, index_bit, jnp.zeros((1, Q_BLOCK), jnp.int32))

    cut = lax.cond(jnp.max(n_ge) > IDX_TOPK, tie_cut, lambda: jnp.full((1, Q_BLOCK), seq, jnp.int32))

    acc_ref[...] = jnp.zeros_like(acc_ref)

    def attend(step, m):
        scores, biases = [], []
        for k0 in tiles(step):
            scores.append(_dot_nt(kc_ref[0, pl.ds(k0, KV_TILE), :], qe_ref[...]))
            key = key_ref[pl.ds(k0, KV_TILE), :]
            k_pos = k0 + k_iota
            sel = (k_pos <= q_pos) & ((key > thr) | ((key == thr) & (k_pos <= cut)))
            biases.append(jnp.where(sel, 0.0, MASK_BIAS))
        m_new, alpha = [], []
        for h in range(N_HEADS):
            sl = slice(h * Q_BLOCK, (h + 1) * Q_BLOCK)
            sh = [s[:, sl] + b for s, b in zip(scores, biases)]
            top = _tree(jnp.maximum, [_fold_rows(x, jnp.maximum) for x in sh])
            m_h = jnp.maximum(m[:, sl], jnp.max(top, axis=0, keepdims=True))
            for t, x in enumerate(sh):
                p_ref[t * KV_TILE:(t + 1) * KV_TILE, sl] = jnp.exp(x - m_h).astype(BF16)
            m_new.append(m_h)
            alpha.append(jnp.exp(m[:, sl] - m_h))
        pv = _dot(ct_ref[0, step], p_ref[...])
        acc_ref[...] = jnp.concatenate(alpha, axis=1) * acc_ref[...] + pv
        return jnp.concatenate(m_new, axis=1)

    lax.fori_loop(0, n_steps, attend, jnp.full((1, cols), M_INIT, F32))
    acc = acc_ref[...]
    o_lat = (acc[:KV_RANK] / acc[KV_RANK:KV_RANK + 1]).astype(BF16)
    out_t = jnp.concatenate([_dot(wuv_ref[h], o_lat[:, h * Q_BLOCK:(h + 1) * Q_BLOCK])
                             for h in range(N_HEADS)], axis=0)
    o_ref[0] = out_t.T.astype(BF16)


def _dsa(qa, qi, wt, kc, ct, ki, wq, wuv):
    b, s, _ = qa.shape
    blk = lambda bi, i: (bi, i, 0)
    res = lambda bi, i: (bi, 0, 0)
    const3 = lambda bi, i: (0, 0, 0)
    cols = N_HEADS * Q_BLOCK
    return pl.pallas_call(
        functools.partial(_dsa_kernel, seq=s),
        out_shape=jax.ShapeDtypeStruct((b, s, WIDTH), BF16),
        grid=(b, s // Q_BLOCK),
        in_specs=[pl.BlockSpec((1, Q_BLOCK, WIDTH), blk),
                  pl.BlockSpec((1, Q_BLOCK, WIDTH), blk),
                  pl.BlockSpec((1, LANES, Q_BLOCK), lambda bi, i: (bi, 0, i)),
                  pl.BlockSpec((1, s, 2 * LANES), res),
                  pl.BlockSpec((1, s // DSA_STEP, CKV_ROWS, DSA_STEP), lambda bi, i: (bi, 0, 0, 0)),
                  pl.BlockSpec((1, s, LANES), res),
                  pl.BlockSpec((N_HEADS, WIDTH, 2 * LANES), const3),
                  pl.BlockSpec((N_HEADS, HEAD_DIM, KV_RANK), const3)],
        out_specs=pl.BlockSpec((1, Q_BLOCK, WIDTH), blk),
        scratch_shapes=[pltpu.VMEM((s, Q_BLOCK), jnp.int32),
                        pltpu.VMEM((s, Q_BLOCK), jnp.int16),
                        pltpu.VMEM((s, Q_BLOCK), jnp.int16),
                        pltpu.VMEM((cols, LANES), BF16),
                        pltpu.VMEM((cols, 2 * LANES), BF16),
                        pltpu.VMEM((DSA_STEP, cols), BF16),
                        pltpu.VMEM((CKV_ROWS, cols), F32)],
        compiler_params=pltpu.CompilerParams(dimension_semantics=("arbitrary", "arbitrary"),
                                             vmem_limit_bytes=VMEM_LIMIT),
        name="dsa",
    )(qa, qi, wt, kc, ct, ki, wq, wuv)


def _moba_kernel(q_ref, k_ref, vt_ref, o_ref, kmean_ref, chosen_ref, *, n_blocks):
    i = pl.program_id(2)
    tq = MOBA_BLOCK

    @pl.when(i == 0)
    def _():
        for n in range(n_blocks):
            kb = k_ref[0, n * MOBA_BLOCK:(n + 1) * MOBA_BLOCK, :].astype(F32)
            kmean_ref[n:n + 1, :] = jnp.mean(kb, axis=0, keepdims=True)

    q = q_ref[0]
    lane = lax.broadcasted_iota(jnp.int32, (1, LANES), 1)
    blk = lax.broadcasted_iota(jnp.int32, (n_blocks, tq), 0).astype(F32)
    own = i.astype(F32)
    n_sel = min(MOBA_TOPK, n_blocks - 1)

    qs = []
    for half in range(2):
        keep = (lane < HEAD_DIM) if half == 0 else (lane >= HEAD_DIM)
        qh = jnp.where(keep, q, jnp.zeros_like(q))
        gate = lax.dot_general(kmean_ref[...], qh.astype(F32), NT_DIMS, preferred_element_type=F32,
                               precision=lax.Precision.HIGHEST)
        gate = jnp.where(blk < own, gate, -jnp.inf)
        chosen = jnp.zeros((n_blocks, tq), F32)
        for _ in range(n_sel):
            best = jnp.max(gate, axis=0, keepdims=True)
            arg = jnp.min(jnp.where(gate == best, blk, float(n_blocks)), axis=0, keepdims=True)
            hit = blk == arg
            chosen = jnp.where(hit & (best > -jnp.inf), 1.0, chosen)
            gate = jnp.where(hit, -jnp.inf, gate)
        chosen_ref[half] = chosen
        qs.append(qh * (HEAD_DIM ** -0.5))

    def step(blocks, carry):
        out = []
        for half in range(2):
            m, acc = carry[half]
            scores = []
            for n, biases in blocks:
                kt = k_ref[0, pl.ds(pl.multiple_of(n * MOBA_BLOCK, MOBA_BLOCK), MOBA_BLOCK), :]
                scores.append(_dot_nt(kt, qs[half]) + biases[half])
            top = _tree(jnp.maximum, [_fold_rows(s, jnp.maximum) for s in scores])
            m_new = jnp.maximum(m, jnp.max(top, axis=0, keepdims=True))
            pv = _tree(jnp.add, [
                _dot(vt_ref[0, 0, n, half * VHEAD_ROWS:(half + 1) * VHEAD_ROWS, :],
                     jnp.exp(s - m_new).astype(BF16))
                for (n, _), s in zip(blocks, scores)])
            out.append((m_new, jnp.exp(m - m_new) * acc + pv))
        return tuple(out)

    def past(g, carry):
        blocks = []
        for u in range(MOBA_GROUP):
            n = g * MOBA_GROUP + u
            n_in = jnp.minimum(n, i)
            live = jnp.where(n < i, 1.0, 0.0)
            biases = [jnp.where(chosen_ref[half, pl.ds(n_in, 1), :] * live > 0.0, 0.0, MASK_BIAS)
                      for half in range(2)]
            blocks.append((n_in, biases))
        return step(blocks, carry)

    init = tuple((jnp.full((1, tq), M_INIT, F32), jnp.zeros((VHEAD_ROWS, tq), F32)) for _ in range(2))
    carry = lax.fori_loop(0, (i + MOBA_GROUP - 1) // MOBA_GROUP, past, init)
    row = lax.broadcasted_iota(jnp.int32, (KV_TILE, tq), 0)
    col = lax.broadcasted_iota(jnp.int32, (KV_TILE, tq), 1)
    causal = jnp.where(row <= col, 0.0, MASK_BIAS)
    carry = step([(i, [causal, causal])], carry)
    out_t = jnp.concatenate([acc[:HEAD_DIM] / acc[HEAD_DIM:HEAD_DIM + 1] for _, acc in carry], axis=0)
    o_ref[0] = out_t.T.astype(BF16)


def _moba(qb, kb, vt):
    b, s, _ = qb.shape
    n_blocks = s // MOBA_BLOCK
    return pl.pallas_call(
        functools.partial(_moba_kernel, n_blocks=n_blocks),
        out_shape=jax.ShapeDtypeStruct((b, s, WIDTH), BF16),
        grid=(b, N_CHUNKS, n_blocks),
        in_specs=[pl.BlockSpec((1, MOBA_BLOCK, LANES), lambda bi, c, i: (bi, i, c)),
                  pl.BlockSpec((1, s, LANES), lambda bi, c, i: (bi, 0, c)),
                  pl.BlockSpec((1, 1, n_blocks, 2 * VHEAD_ROWS, KV_TILE), lambda bi, c, i: (bi, c, 0, 0, 0))],
        out_specs=pl.BlockSpec((1, MOBA_BLOCK, LANES), lambda bi, c, i: (bi, i, c)),
        scratch_shapes=[pltpu.VMEM((n_blocks, LANES), F32),
                        pltpu.VMEM((2, n_blocks, MOBA_BLOCK), F32)],
        compiler_params=pltpu.CompilerParams(
            dimension_semantics=("arbitrary", "arbitrary", "arbitrary"),
            vmem_limit_bytes=VMEM_LIMIT),
        name="moba",
    )(qb, kb, vt)


def _merge_kernel(x_ref, oa_ref, ob_ref, mod_ref, g_ref, wg_ref, wa_ref, wb_ref, wo_ref, o_ref):
    x = x_ref[0]
    mod = mod_ref[0]
    hb = _norm_mod(x, g_ref[...], mod[3:4], mod[4:5]).astype(BF16)
    gates = _dot(hb, wg_ref[...])
    ya = _dot(oa_ref[0], wa_ref[...])
    yb = _dot(ob_ref[0], wb_ref[...])
    y = jax.nn.sigmoid(gates[:, :D_MODEL]) * ya + jax.nn.sigmoid(gates[:, D_MODEL:]) * yb
    o_ref[0] = x + mod[5:6] * _dot(y.astype(BF16), wo_ref[...])


def _merge(x, oa, ob, mod, norm_g, w_gate, w_a, w_b, w_o, *, tm=512):
    b, s, d = x.shape
    const = lambda bi, i: (0, 0)
    tok = lambda bi, i: (bi, i, 0)
    return pl.pallas_call(
        _merge_kernel,
        out_shape=jax.ShapeDtypeStruct(x.shape, F32),
        grid=(b, s // tm),
        in_specs=[pl.BlockSpec((1, tm, d), tok),
                  pl.BlockSpec((1, tm, WIDTH), tok),
                  pl.BlockSpec((1, tm, WIDTH), tok),
                  pl.BlockSpec((1, N_MOD, d), lambda bi, i: (bi, 0, 0)),
                  pl.BlockSpec((1, d), const),
                  pl.BlockSpec((d, 2 * d), const),
                  pl.BlockSpec((WIDTH, d), const),
                  pl.BlockSpec((WIDTH, d), const),
                  pl.BlockSpec((d, d), const)],
        out_specs=pl.BlockSpec((1, tm, d), tok),
        compiler_params=pltpu.CompilerParams(dimension_semantics=("arbitrary", "arbitrary"),
                                             vmem_limit_bytes=VMEM_LIMIT),
        name="merge",
    )(x, oa, ob, mod, norm_g.reshape(1, d), w_gate, w_a, w_b, w_o)


def _pack_proj_weights(w_in):
    splits = (WIDTH, KV_RANK, ROPE_DIM, WIDTH, HEAD_DIM, N_HEADS, WIDTH, WIDTH, WIDTH, D_MODEL, D_MODEL)
    parts, off = [], 0
    for w in splits:
        parts.append(w_in[:, off:off + w])
        off += w
    q_a, ckv, k_rope, q_idx, k_idx, w_idx, q_b, k_b, v_b, gate_a, gate_b = parts
    pad = jnp.zeros((w_in.shape[0], LANES - ROPE_DIM - N_HEADS), w_in.dtype)
    w_proj = jnp.concatenate([q_a, q_idx, q_b, k_b, v_b, ckv, k_rope, w_idx, pad, k_idx, k_idx], axis=1)
    w_gate = jnp.concatenate([gate_a, gate_b], axis=1)
    return w_proj.astype(BF16), w_gate.astype(BF16)


def _pack_dsa_weights(w_uk, w_uv):
    scale = HEAD_DIM ** -0.5
    wq = jnp.zeros((N_HEADS, WIDTH, 2 * LANES), F32)
    eye = jnp.eye(ROPE_DIM, dtype=F32)
    for h in range(N_HEADS):
        lo = h * HEAD_DIM
        wq = wq.at[h, lo + ROPE_DIM:lo + HEAD_DIM, :KV_RANK].set(w_uk[:, h, :].T * scale)
        wq = wq.at[h, lo:lo + ROPE_DIM, KV_RANK:KV_RANK + ROPE_DIM].set(eye * scale)
    wuv = jnp.transpose(w_uv, (1, 2, 0))
    return wq.astype(BF16), wuv.astype(BF16)


def _rope_inv_freq():
    half = ROPE_HALF
    inv_freq = jnp.power(ROPE_THETA, -jnp.arange(half, dtype=F32) / half)
    lane = jnp.arange(LANES)
    row = jnp.where(lane % HEAD_DIM < ROPE_DIM, inv_freq[lane % half], 0.0)
    return row.reshape(1, LANES).astype(F32)


def kernel(x, c, positions, ada_w, ada_b, norm1_g, ffn1_w_in, ffn1_w_out, norm2_g, w_in, kv_norm_g,
           w_uk, w_uv, w_branch_a, w_branch_b, w_out, norm3_g, ffn2_w_in, ffn2_w_out, final_g):
    depth = ada_w.shape[0]
    pos3 = positions[..., None]
    invf = _rope_inv_freq()
    for l in range(depth):
        mod = _adaln(c, ada_w[l], ada_b[l])
        x = _ffn(x, mod, norm1_g[l], ffn1_w_in[l].astype(BF16), ffn1_w_out[l].astype(BF16), final_g,
                 mod_row=0, final=False)
        w_proj, w_gate = _pack_proj_weights(w_in[l])
        wq, wuv = _pack_dsa_weights(w_uk[l], w_uv[l])
        qa, qi, qb, kb, vt, kc, ct, ki, wt = _proj(x, pos3, mod, norm2_g[l], w_proj, kv_norm_g[l], invf)
        oa = _dsa(qa, qi, wt, kc, ct, ki, wq, wuv)
        ob = _moba(qb, kb, vt)
        x = _merge(x, oa, ob, mod, norm2_g[l], w_gate, w_branch_a[l].astype(BF16),
                   w_branch_b[l].astype(BF16), w_out[l].astype(BF16))
        last = l == depth - 1
        x = _ffn(x, mod, norm3_g[l], ffn2_w_in[l].astype(BF16), ffn2_w_out[l].astype(BF16), final_g,
                 mod_row=6, final=last)
    return x
```

```python
import functools

import jax
import jax.numpy as jnp
from jax import lax
from jax.experimental import pallas as pl
from jax.experimental.pallas import tpu as pltpu

D_MODEL = 1024
N_HEADS = 8
HEAD_DIM = 64
ROPE_DIM = 16
NOPE_DIM = HEAD_DIM - ROPE_DIM
KV_RANK = 128
IDX_TOPK = 256
MOBA_BLOCK = 256
MOBA_TOPK = 3
Q_BLOCK = 128
D_FF = 2816
ROPE_THETA = 500000.0
EPS = 1e-6
N_MOD = 9
WIDTH = N_HEADS * HEAD_DIM

LANES = 128
SUBLANES = 8
BF16_ROWS = 16
KV_TILE = 256
VMEM_LIMIT = 52 * 1024 * 1024

F32 = jnp.float32
BF16 = jnp.bfloat16
INT_MIN = -(2 ** 31)
HALF16 = 2 ** 15
LOG2E = 1.4426950408889634
MASK_BIAS = -2e30
M_INIT = -1e30
NT_DIMS = (((1,), (1,)), ((), ()))

CKV_ROWS = KV_RANK + BF16_ROWS
VHEAD_ROWS = HEAD_DIM + BF16_ROWS

DSA_STEP = 2 * KV_TILE
MOBA_GROUP = 4


def _dot(a, b):
    return jnp.dot(a, b, preferred_element_type=F32)


def _dot_nt(a, b):
    return lax.dot_general(a, b, NT_DIMS, preferred_element_type=F32)


def _norm_mod(x, g, shift, scale):
    y = x * lax.rsqrt(jnp.mean(x * x, axis=-1, keepdims=True) + EPS)
    return (y * g) * (1.0 + scale) + shift


def _exp2_bf16(d):
    return jnp.exp2(d.astype(BF16))


def _tree(op, parts):
    parts = list(parts)
    while len(parts) > 1:
        nxt = [op(parts[k], parts[k + 1]) for k in range(0, len(parts) - 1, 2)]
        if len(parts) % 2:
            nxt.append(parts[-1])
        parts = nxt
    return parts[0]


def _fold_rows(x, op):
    return _tree(op, [x[k:k + SUBLANES] for k in range(0, x.shape[0], SUBLANES)])


def _ones_rows(cols):
    row = lax.broadcasted_iota(jnp.int32, (BF16_ROWS, cols), 0)
    return jnp.where(row == 0, 1.0, 0.0).astype(BF16)


def _adaln_kernel(c_ref, w_ref, b_ref, o_ref):
    c = c_ref[...]
    act = c * jax.nn.sigmoid(c)
    o_ref[...] = jnp.dot(act, w_ref[...], preferred_element_type=F32,
                         precision=lax.Precision.HIGHEST) + b_ref[...]


def _adaln(c, ada_w, ada_b):
    b = c.shape[0]
    n = ada_w.shape[1]
    rows = SUBLANES
    c_pad = jnp.zeros((rows, D_MODEL), F32).at[:b].set(c)
    tn = 1536
    out = pl.pallas_call(
        _adaln_kernel,
        out_shape=jax.ShapeDtypeStruct((rows, n), F32),
        grid=(n // tn,),
        in_specs=[pl.BlockSpec((rows, D_MODEL), lambda j: (0, 0)),
                  pl.BlockSpec((D_MODEL, tn), lambda j: (0, j)),
                  pl.BlockSpec((1, tn), lambda j: (0, j))],
        out_specs=pl.BlockSpec((rows, tn), lambda j: (0, j)),
        compiler_params=pltpu.CompilerParams(dimension_semantics=("arbitrary",),
                                             vmem_limit_bytes=VMEM_LIMIT),
        name="adaln",
    )(c_pad, ada_w, ada_b.reshape(1, n))
    return out[:b].reshape(b, N_MOD, D_MODEL)


FF_CHUNK = D_FF // 2


def _ffn_kernel(x_ref, mod_ref, g_ref, win_ref, wout_ref, fg_ref, o_ref, *, mod_row, final):
    x = x_ref[0]
    mod = mod_ref[0]
    shift, scale, gate = (mod[mod_row + k:mod_row + k + 1] for k in range(3))
    hb = _norm_mod(x, g_ref[...], shift, scale).astype(BF16)
    acc = jnp.zeros(x.shape, F32)
    for j in range(D_FF // FF_CHUNK):
        lo = j * FF_CHUNK
        a = _dot(hb, win_ref[:, lo:lo + FF_CHUNK])
        b = _dot(hb, win_ref[:, D_FF + lo:D_FF + lo + FF_CHUNK])
        act = (a * jax.nn.sigmoid(a) * b).astype(BF16)
        acc = acc + _dot(act, wout_ref[lo:lo + FF_CHUNK, :])
    y = x + (0.5 * gate) * acc
    if final:
        y = (y * lax.rsqrt(jnp.mean(y * y, axis=-1, keepdims=True) + EPS)) * fg_ref[...]
    o_ref[0] = y


def _ffn(x, mod, norm_g, w_in, w_out, final_g, *, mod_row, final, tm=512):
    b, s, d = x.shape
    const = lambda bi, i: (0, 0)
    single = pl.Buffered(1)
    return pl.pallas_call(
        functools.partial(_ffn_kernel, mod_row=mod_row, final=final),
        out_shape=jax.ShapeDtypeStruct(x.shape, F32),
        grid=(b, s // tm),
        in_specs=[pl.BlockSpec((1, tm, d), lambda bi, i: (bi, i, 0)),
                  pl.BlockSpec((1, N_MOD, d), lambda bi, i: (bi, 0, 0)),
                  pl.BlockSpec((1, d), const),
                  pl.BlockSpec((d, 2 * D_FF), const, pipeline_mode=single),
                  pl.BlockSpec((D_FF, d), const, pipeline_mode=single),
                  pl.BlockSpec((1, d), const)],
        out_specs=pl.BlockSpec((1, tm, d), lambda bi, i: (bi, i, 0)),
        compiler_params=pltpu.CompilerParams(dimension_semantics=("arbitrary", "arbitrary"),
                                             vmem_limit_bytes=VMEM_LIMIT),
        name="ffn_final" if final else "ffn",
    )(x, mod, norm_g.reshape(1, d), w_in, w_out, final_g.reshape(1, d))


PROJ_COLS = 5 * WIDTH + 3 * LANES
MISC_W_ROW = ROPE_DIM
ROPE_HALF = ROPE_DIM // 2
N_CHUNKS = WIDTH // LANES


def _proj_kernel(x_ref, pos_ref, mod_ref, g_ref, w_ref, kvg_ref, invf_ref,
                 qa_ref, qi_ref, qb_ref, kb_ref, vt_ref, kc_ref, ct_ref, ki_ref, wt_ref):
    x = x_ref[0]
    tm = x.shape[0]
    mod = mod_ref[0]
    hb = _norm_mod(x, g_ref[...], mod[3:4], mod[4:5]).astype(BF16)
    p = _dot(hb, w_ref[...])

    ang = pos_ref[0].astype(F32) * invf_ref[...]
    cos, sin = jnp.cos(ang), jnp.sin(ang)
    lane = lax.broadcasted_iota(jnp.int32, (1, LANES), 1)
    sin_lo = jnp.where(lane % HEAD_DIM < ROPE_HALF, -sin, 0.0)
    sin_hi = jnp.where(lane % HEAD_DIM >= ROPE_HALF, sin, 0.0)

    def rot(xc):
        return (xc * cos + pltpu.roll(xc, LANES - ROPE_HALF, 1) * sin_lo
                + pltpu.roll(xc, ROPE_HALF, 1) * sin_hi)

    def chunk(k):
        return p[:, k * LANES:(k + 1) * LANES]

    for out_ref, base, scale in ((qa_ref, 0, None), (qi_ref, N_CHUNKS, None),
                                 (qb_ref, 2 * N_CHUNKS, LOG2E * HEAD_DIM ** -0.5), (kb_ref, 3 * N_CHUNKS, None)):
        for k in range(N_CHUNKS):
            r = rot(chunk(base + k))
            out_ref[0, :, k * LANES:(k + 1) * LANES] = (r if scale is None else r * scale).astype(BF16)

    ones = _ones_rows(KV_TILE)
    for k in range(N_CHUNKS):
        vc = chunk(4 * N_CHUNKS + k)
        for t in range(tm // KV_TILE):
            vt = vc[t * KV_TILE:(t + 1) * KV_TILE].T.astype(BF16)
            for half in range(2):
                r0 = half * VHEAD_ROWS
                vt_ref[0, k, t, r0:r0 + HEAD_DIM, :] = vt[half * HEAD_DIM:(half + 1) * HEAD_DIM]
                vt_ref[0, k, t, r0 + HEAD_DIM:r0 + VHEAD_ROWS, :] = ones

    ckv = chunk(5 * N_CHUNKS)
    ckv = (ckv * lax.rsqrt(jnp.mean(ckv * ckv, axis=-1, keepdims=True) + EPS)) * kvg_ref[...]
    misc = rot(chunk(5 * N_CHUNKS + 1))
    kc_ref[0, :, :LANES] = ckv.astype(BF16)
    kc_ref[0, :, LANES:] = jnp.where(lane < ROPE_DIM, misc, 0.0).astype(BF16)
    for t in range(tm // DSA_STEP):
        ct_ref[0, t, :KV_RANK, :] = ckv[t * DSA_STEP:(t + 1) * DSA_STEP].T.astype(BF16)
        ct_ref[0, t, KV_RANK:, :] = _ones_rows(DSA_STEP)
    ki_ref[0] = rot(chunk(5 * N_CHUNKS + 2)).astype(BF16)
    wt_ref[0] = misc.T


def _proj(x, pos3, mod, norm_g, w_proj, kv_g, invf, *, tm=512):
    b, s, d = x.shape
    const = lambda bi, i: (0, 0)
    tok = lambda bi, i: (bi, i, 0)
    wide = jax.ShapeDtypeStruct((b, s, WIDTH), BF16)
    n_kt = s // KV_TILE
    t_kt = tm // KV_TILE
    return pl.pallas_call(
        _proj_kernel,
        out_shape=(wide, wide, wide, wide,
                   jax.ShapeDtypeStruct((b, N_CHUNKS, n_kt, 2 * VHEAD_ROWS, KV_TILE), BF16),
                   jax.ShapeDtypeStruct((b, s, 2 * LANES), BF16),
                   jax.ShapeDtypeStruct((b, s // DSA_STEP, CKV_ROWS, DSA_STEP), BF16),
                   jax.ShapeDtypeStruct((b, s, LANES), BF16),
                   jax.ShapeDtypeStruct((b, LANES, s), F32)),
        grid=(b, s // tm),
        in_specs=[pl.BlockSpec((1, tm, d), tok),
                  pl.BlockSpec((1, tm, 1), tok),
                  pl.BlockSpec((1, N_MOD, d), lambda bi, i: (bi, 0, 0)),
                  pl.BlockSpec((1, d), const),
                  pl.BlockSpec((d, PROJ_COLS), const),
                  pl.BlockSpec((1, KV_RANK), const),
                  pl.BlockSpec((1, LANES), const)],
        out_specs=(pl.BlockSpec((1, tm, WIDTH), tok),) * 4 + (
            pl.BlockSpec((1, N_CHUNKS, t_kt, 2 * VHEAD_ROWS, KV_TILE), lambda bi, i: (bi, 0, i, 0, 0)),
            pl.BlockSpec((1, tm, 2 * LANES), tok),
            pl.BlockSpec((1, tm // DSA_STEP, CKV_ROWS, DSA_STEP), lambda bi, i: (bi, i, 0, 0)),
            pl.BlockSpec((1, tm, LANES), tok),
            pl.BlockSpec((1, LANES, tm), lambda bi, i: (bi, 0, i))),
        compiler_params=pltpu.CompilerParams(dimension_semantics=("arbitrary", "arbitrary"),
                                             vmem_limit_bytes=VMEM_LIMIT),
        name="proj",
    )(x, pos3, mod, norm_g.reshape(1, d), w_proj, kv_g.reshape(1, KV_RANK), invf)


def _dsa_kernel(qa_ref, qi_ref, wt_ref, kc_ref, ct_ref, ki_ref, wq_ref, wuv_ref, o_ref,
                key_ref, hi_ref, lo_ref, ql_ref, qe_ref, p_ref, acc_ref, sa_ref, sb_ref, *, seq):
    i = pl.program_id(1)
    n_pairs = (i * Q_BLOCK + Q_BLOCK + 2 * DSA_STEP - 1) // (2 * DSA_STEP)
    n_steps = 2 * n_pairs
    sub = DSA_STEP // KV_TILE
    cols = N_HEADS * Q_BLOCK

    qi = qi_ref[0]
    qa = qa_ref[0]
    lane = lax.broadcasted_iota(jnp.int32, (1, LANES), 1)
    for h in range(N_HEADS):
        c, half = divmod(h, 2)
        xc = qi[:, c * LANES:(c + 1) * LANES]
        keep = (lane < HEAD_DIM) if half == 0 else (lane >= HEAD_DIM)
        ql_ref[h * Q_BLOCK:(h + 1) * Q_BLOCK, :] = jnp.where(keep, xc, jnp.zeros_like(xc))
        qe_ref[h * Q_BLOCK:(h + 1) * Q_BLOCK, :] = (_dot(qa, wq_ref[h]) * LOG2E).astype(BF16)
    ws = (wt_ref[0] * (N_HEADS ** -0.5)) * (HEAD_DIM ** -0.5)

    k_iota = lax.broadcasted_iota(jnp.int32, (KV_TILE, Q_BLOCK), 0)
    q_pos = i * Q_BLOCK + lax.broadcasted_iota(jnp.int32, (KV_TILE, Q_BLOCK), 1)

    def tiles(step):
        return [pl.multiple_of(step * DSA_STEP + t * KV_TILE, KV_TILE) for t in range(sub)]

    def matmul_step(lhs_ref, rhs_ref, step, dst_ref):
        for t, k0 in enumerate(tiles(jnp.minimum(step, n_steps - 1))):
            dst_ref[t * KV_TILE:(t + 1) * KV_TILE, :] = _dot_nt(lhs_ref[0, pl.ds(k0, KV_TILE), :], rhs_ref[...])

    def pipelined(lhs_ref, rhs_ref, consume, init):
        matmul_step(lhs_ref, rhs_ref, 0, sa_ref)

        def pair(j, carry):
            matmul_step(lhs_ref, rhs_ref, 2 * j + 1, sb_ref)
            carry = consume(2 * j, sa_ref, carry)
            matmul_step(lhs_ref, rhs_ref, 2 * j + 2, sa_ref)
            return consume(2 * j + 1, sb_ref, carry)

        return lax.fori_loop(0, n_pairs, pair, init)

    def score_step(step, logits_ref, carry):
        for t, k0 in enumerate(tiles(step)):
            rows = slice(t * KV_TILE, (t + 1) * KV_TILE)
            sc = _tree(jnp.add, [jnp.maximum(logits_ref[rows, h * Q_BLOCK:(h + 1) * Q_BLOCK], 0.0)
                                 * ws[MISC_W_ROW + h:MISC_W_ROW + h + 1] for h in range(N_HEADS)])
            bits = pltpu.bitcast(sc, jnp.int32)
            key = bits ^ ((bits >> 31) & jnp.int32(0x7FFFFFFF))
            key = jnp.where(k0 + k_iota <= q_pos, key, jnp.int32(INT_MIN))
            key_ref[pl.ds(k0, KV_TILE), :] = key
            hi_ref[pl.ds(k0, KV_TILE), :] = (key >> 16).astype(jnp.int16)
            lo_ref[pl.ds(k0, KV_TILE), :] = ((key & jnp.int32(0xFFFF)) - HALF16).astype(jnp.int16)
        return carry

    pipelined(ki_ref, ql_ref, score_step, 0)

    def count(pred):
        def body(step, acc):
            hits = [_fold_rows(jnp.where(pred(key_ref[pl.ds(k0, KV_TILE), :], k0), 1.0, 0.0), jnp.add)
                    for k0 in tiles(step)]
            return acc + _tree(jnp.add, hits)
        acc = lax.fori_loop(0, n_steps, body, jnp.zeros((SUBLANES, Q_BLOCK), F32))
        return jnp.sum(acc, axis=0, keepdims=True)

    one16 = jnp.ones((BF16_ROWS, Q_BLOCK), jnp.int16)
    zero16 = jnp.zeros((BF16_ROWS, Q_BLOCK), jnp.int16)

    def rows16(v):
        return jnp.broadcast_to(v, (BF16_ROWS, Q_BLOCK)).astype(jnp.int16)

    low_min = jnp.full((KV_TILE, Q_BLOCK), -HALF16, jnp.int16)

    def count16(ref, pred):
        def body(pair, acc):
            hits = []
            for k0 in tiles(2 * pair) + tiles(2 * pair + 1):
                t = ref[pl.ds(k0, KV_TILE), :]
                hits += [jnp.where(pred(t[r:r + BF16_ROWS]), one16, zero16)
                         for r in range(0, KV_TILE, BF16_ROWS)]
            return acc + _tree(jnp.add, hits)
        acc = lax.fori_loop(0, n_pairs, body, zero16)
        return jnp.sum(acc.astype(F32), axis=0, keepdims=True)

    def half_bisect(ref, target):
        def bit(b, t_u):
            cand = t_u | jnp.left_shift(jnp.int32(1), 15 - b)
            c16 = rows16(cand - HALF16)
            cnt = count16(ref, lambda t: t >= c16)
            return jnp.where(cnt >= target, cand, t_u)
        return lax.fori_loop(0, 16, bit, jnp.zeros((1, Q_BLOCK), jnp.int32))

    hi_s = half_bisect(hi_ref, float(IDX_TOPK)) - HALF16
    hi16 = rows16(hi_s)
    slots = IDX_TOPK - count16(hi_ref, lambda t: t > hi16)

    hi_tile = jnp.broadcast_to(hi_s, (KV_TILE, Q_BLOCK)).astype(jnp.int16)

    def mask_low(step, carry):
        for k0 in tiles(step):
            sl = pl.ds(k0, KV_TILE)
            lo_ref[sl, :] = jnp.where(hi_ref[sl, :] == hi_tile, lo_ref[sl, :], low_min)
        return carry

    lax.fori_loop(0, n_steps, mask_low, 0)
    lo_u = half_bisect(lo_ref, slots)
    thr = jnp.maximum(hi_s * (2 * HALF16) + lo_u, jnp.int32(INT_MIN + 1))

    def attend_all(selected):
        acc_ref[...] = jnp.zeros_like(acc_ref)

        def attend(step, s_ref, m):
            biases = [jnp.where(selected(key_ref[pl.ds(k0, KV_TILE), :], k0), 0.0, MASK_BIAS)
                      for k0 in tiles(step)]
            m_new = []
            for h in range(N_HEADS):
                sl = slice(h * Q_BLOCK, (h + 1) * Q_BLOCK)
                sh = [s_ref[t * KV_TILE:(t + 1) * KV_TILE, sl] + b for t, b in enumerate(biases)]
                top = _tree(jnp.maximum, [_fold_rows(x, jnp.maximum) for x in sh])
                m_new.append(jnp.maximum(m[:, sl], jnp.max(top, axis=0, keepdims=True)))
                for t, x in enumerate(sh):
                    p_ref[t * KV_TILE:(t + 1) * KV_TILE, sl] = _exp2_bf16(x - m_new[h])
                if h % 2:
                    pair = slice((h - 1) * Q_BLOCK, (h + 1) * Q_BLOCK)
                    pv = _dot(ct_ref[0, step], p_ref[:, pair])
                    alpha = jnp.exp2(m[:, pair] - jnp.concatenate(m_new[h - 1:h + 1], axis=1))
                    acc_ref[:, pair] = alpha * acc_ref[:, pair] + pv
            return jnp.concatenate(m_new, axis=1)

        pipelined(kc_ref, qe_ref, attend, jnp.full((1, cols), M_INIT, F32))

    idx_bits = max(1, (seq - 1).bit_length())

    def attend_with_ties():
        need = IDX_TOPK - count(lambda k, k0: k > thr)

        def index_bit(b, cut):
            cand = cut | jnp.left_shift(jnp.int32(1), idx_bits - 1 - b)
            cnt = count(lambda k, k0: (k == thr) & (k0 + k_iota < cand))
            return jnp.where(cnt < need, cand, cut)

        cut = lax.fori_loop(0, idx_bits, index_bit, jnp.zeros((1, Q_BLOCK), jnp.int32))
        attend_all(lambda k, k0: (k > thr) | ((k == thr) & (k0 + k_iota <= cut)))

    lax.cond(jnp.max(count(lambda k, k0: k >= thr)) > IDX_TOPK,
             attend_with_ties, lambda: attend_all(lambda k, k0: k >= thr))
    acc = acc_ref[...]
    o_lat = (acc[:KV_RANK] / acc[KV_RANK:KV_RANK + 1]).astype(BF16)
    out_t = jnp.concatenate([_dot(wuv_ref[h], o_lat[:, h * Q_BLOCK:(h + 1) * Q_BLOCK])
                             for h in range(N_HEADS)], axis=0)
    o_ref[0] = out_t.T.astype(BF16)


def _dsa(qa, qi, wt, kc, ct, ki, wq, wuv):
    b, s, _ = qa.shape
    blk = lambda bi, i: (bi, i, 0)
    res = lambda bi, i: (bi, 0, 0)
    const3 = lambda bi, i: (0, 0, 0)
    cols = N_HEADS * Q_BLOCK
    return pl.pallas_call(
        functools.partial(_dsa_kernel, seq=s),
        out_shape=jax.ShapeDtypeStruct((b, s, WIDTH), BF16),
        grid=(b, s // Q_BLOCK),
        in_specs=[pl.BlockSpec((1, Q_BLOCK, WIDTH), blk),
                  pl.BlockSpec((1, Q_BLOCK, WIDTH), blk),
                  pl.BlockSpec((1, LANES, Q_BLOCK), lambda bi, i: (bi, 0, i)),
                  pl.BlockSpec((1, s, 2 * LANES), res),
                  pl.BlockSpec((1, s // DSA_STEP, CKV_ROWS, DSA_STEP), lambda bi, i: (bi, 0, 0, 0)),
                  pl.BlockSpec((1, s, LANES), res),
                  pl.BlockSpec((N_HEADS, WIDTH, 2 * LANES), const3),
                  pl.BlockSpec((N_HEADS, HEAD_DIM, KV_RANK), const3)],
        out_specs=pl.BlockSpec((1, Q_BLOCK, WIDTH), blk),
        scratch_shapes=[pltpu.VMEM((s, Q_BLOCK), jnp.int32),
                        pltpu.VMEM((s, Q_BLOCK), jnp.int16),
                        pltpu.VMEM((s, Q_BLOCK), jnp.int16),
                        pltpu.VMEM((cols, LANES), BF16),
                        pltpu.VMEM((cols, 2 * LANES), BF16),
                        pltpu.VMEM((DSA_STEP, cols), BF16),
                        pltpu.VMEM((CKV_ROWS, cols), F32),
                        pltpu.VMEM((DSA_STEP, cols), F32),
                        pltpu.VMEM((DSA_STEP, cols), F32)],
        compiler_params=pltpu.CompilerParams(dimension_semantics=("arbitrary", "arbitrary"),
                                             vmem_limit_bytes=VMEM_LIMIT),
        name="dsa",
    )(qa, qi, wt, kc, ct, ki, wq, wuv)


def _moba_kernel(q_ref, k_ref, vt_ref, o_ref, kmean_ref, chosen_ref, *, n_blocks):
    i = pl.program_id(2)
    tq = MOBA_BLOCK

    @pl.when(i == 0)
    def _():
        for n in range(n_blocks):
            kb = k_ref[0, n * MOBA_BLOCK:(n + 1) * MOBA_BLOCK, :].astype(F32)
            kmean_ref[n:n + 1, :] = jnp.mean(kb, axis=0, keepdims=True)

    q = q_ref[0]
    lane = lax.broadcasted_iota(jnp.int32, (1, LANES), 1)
    blk = lax.broadcasted_iota(jnp.int32, (n_blocks, tq), 0).astype(F32)
    own = i.astype(F32)
    n_sel = min(MOBA_TOPK, n_blocks - 1)

    qs = []
    for half in range(2):
        keep = (lane < HEAD_DIM) if half == 0 else (lane >= HEAD_DIM)
        qh = jnp.where(keep, q, jnp.zeros_like(q))
        gate = lax.dot_general(kmean_ref[...], qh.astype(F32), NT_DIMS, preferred_element_type=F32,
                               precision=lax.Precision.HIGHEST)
        gate = jnp.where(blk < own, gate, -jnp.inf)
        chosen = jnp.zeros((n_blocks, tq), F32)
        for _ in range(n_sel):
            best = jnp.max(gate, axis=0, keepdims=True)
            arg = jnp.min(jnp.where(gate == best, blk, float(n_blocks)), axis=0, keepdims=True)
            hit = blk == arg
            chosen = jnp.where(hit & (best > -jnp.inf), 1.0, chosen)
            gate = jnp.where(hit, -jnp.inf, gate)
        chosen_ref[half] = chosen
        qs.append(qh)

    def step(blocks, carry):
        out = []
        for half in range(2):
            m, acc = carry[half]
            scores = []
            for n, biases in blocks:
                kt = k_ref[0, pl.ds(pl.multiple_of(n * MOBA_BLOCK, MOBA_BLOCK), MOBA_BLOCK), :]
                scores.append(_dot_nt(kt, qs[half]) + biases[half])
            top = _tree(jnp.maximum, [_fold_rows(s, jnp.maximum) for s in scores])
            m_new = jnp.maximum(m, jnp.max(top, axis=0, keepdims=True))
            pv = _tree(jnp.add, [
                _dot(vt_ref[0, 0, n, half * VHEAD_ROWS:(half + 1) * VHEAD_ROWS, :],
                     _exp2_bf16(s - m_new))
                for (n, _), s in zip(blocks, scores)])
            out.append((m_new, jnp.exp2(m - m_new) * acc + pv))
        return tuple(out)

    def past(g, carry):
        blocks = []
        for u in range(MOBA_GROUP):
            n = g * MOBA_GROUP + u
            n_in = jnp.minimum(n, i)
            live = jnp.where(n < i, 1.0, 0.0)
            biases = [jnp.where(chosen_ref[half, pl.ds(n_in, 1), :] * live > 0.0, 0.0, MASK_BIAS)
                      for half in range(2)]
            blocks.append((n_in, biases))
        return step(blocks, carry)

    init = tuple((jnp.full((1, tq), M_INIT, F32), jnp.zeros((VHEAD_ROWS, tq), F32)) for _ in range(2))
    carry = lax.fori_loop(0, (i + MOBA_GROUP - 1) // MOBA_GROUP, past, init)
    row = lax.broadcasted_iota(jnp.int32, (KV_TILE, tq), 0)
    col = lax.broadcasted_iota(jnp.int32, (KV_TILE, tq), 1)
    causal = jnp.where(row <= col, 0.0, MASK_BIAS)
    carry = step([(i, [causal, causal])], carry)
    out_t = jnp.concatenate([acc[:HEAD_DIM] / acc[HEAD_DIM:HEAD_DIM + 1] for _, acc in carry], axis=0)
    o_ref[0] = out_t.T.astype(BF16)


def _moba(qb, kb, vt):
    b, s, _ = qb.shape
    n_blocks = s // MOBA_BLOCK
    return pl.pallas_call(
        functools.partial(_moba_kernel, n_blocks=n_blocks),
        out_shape=jax.ShapeDtypeStruct((b, s, WIDTH), BF16),
        grid=(b, N_CHUNKS, n_blocks),
        in_specs=[pl.BlockSpec((1, MOBA_BLOCK, LANES), lambda bi, c, i: (bi, i, c)),
                  pl.BlockSpec((1, s, LANES), lambda bi, c, i: (bi, 0, c)),
                  pl.BlockSpec((1, 1, n_blocks, 2 * VHEAD_ROWS, KV_TILE), lambda bi, c, i: (bi, c, 0, 0, 0))],
        out_specs=pl.BlockSpec((1, MOBA_BLOCK, LANES), lambda bi, c, i: (bi, i, c)),
        scratch_shapes=[pltpu.VMEM((n_blocks, LANES), F32),
                        pltpu.VMEM((2, n_blocks, MOBA_BLOCK), F32)],
        compiler_params=pltpu.CompilerParams(
            dimension_semantics=("arbitrary", "arbitrary", "arbitrary"),
            vmem_limit_bytes=VMEM_LIMIT),
        name="moba",
    )(qb, kb, vt)


def _merge_kernel(x_ref, oa_ref, ob_ref, mod_ref, g_ref, wg_ref, wa_ref, wb_ref, wo_ref, o_ref):
    x = x_ref[0]
    mod = mod_ref[0]
    hb = _norm_mod(x, g_ref[...], mod[3:4], mod[4:5]).astype(BF16)
    gates = _dot(hb, wg_ref[...])
    ya = _dot(oa_ref[0], wa_ref[...])
    yb = _dot(ob_ref[0], wb_ref[...])
    y = jax.nn.sigmoid(gates[:, :D_MODEL]) * ya + jax.nn.sigmoid(gates[:, D_MODEL:]) * yb
    o_ref[0] = x + mod[5:6] * _dot(y.astype(BF16), wo_ref[...])


def _merge(x, oa, ob, mod, norm_g, w_gate, w_a, w_b, w_o, *, tm=512):
    b, s, d = x.shape
    const = lambda bi, i: (0, 0)
    tok = lambda bi, i: (bi, i, 0)
    return pl.pallas_call(
        _merge_kernel,
        out_shape=jax.ShapeDtypeStruct(x.shape, F32),
        grid=(b, s // tm),
        in_specs=[pl.BlockSpec((1, tm, d), tok),
                  pl.BlockSpec((1, tm, WIDTH), tok),
                  pl.BlockSpec((1, tm, WIDTH), tok),
                  pl.BlockSpec((1, N_MOD, d), lambda bi, i: (bi, 0, 0)),
                  pl.BlockSpec((1, d), const),
                  pl.BlockSpec((d, 2 * d), const),
                  pl.BlockSpec((WIDTH, d), const),
                  pl.BlockSpec((WIDTH, d), const),
                  pl.BlockSpec((d, d), const)],
        out_specs=pl.BlockSpec((1, tm, d), tok),
        compiler_params=pltpu.CompilerParams(dimension_semantics=("arbitrary", "arbitrary"),
                                             vmem_limit_bytes=VMEM_LIMIT),
        name="merge",
    )(x, oa, ob, mod, norm_g.reshape(1, d), w_gate, w_a, w_b, w_o)


def _pack_proj_weights(w_in):
    splits = (WIDTH, KV_RANK, ROPE_DIM, WIDTH, HEAD_DIM, N_HEADS, WIDTH, WIDTH, WIDTH, D_MODEL, D_MODEL)
    parts, off = [], 0
    for w in splits:
        parts.append(w_in[:, off:off + w])
        off += w
    q_a, ckv, k_rope, q_idx, k_idx, w_idx, q_b, k_b, v_b, gate_a, gate_b = parts
    pad = jnp.zeros((w_in.shape[0], LANES - ROPE_DIM - N_HEADS), w_in.dtype)
    w_proj = jnp.concatenate([q_a, q_idx, q_b, k_b, v_b, ckv, k_rope, w_idx, pad, k_idx, k_idx], axis=1)
    w_gate = jnp.concatenate([gate_a, gate_b], axis=1)
    return w_proj.astype(BF16), w_gate.astype(BF16)


def _pack_dsa_weights(w_uk, w_uv):
    scale = HEAD_DIM ** -0.5
    wq = jnp.zeros((N_HEADS, WIDTH, 2 * LANES), F32)
    eye = jnp.eye(ROPE_DIM, dtype=F32)
    for h in range(N_HEADS):
        lo = h * HEAD_DIM
        wq = wq.at[h, lo + ROPE_DIM:lo + HEAD_DIM, :KV_RANK].set(w_uk[:, h, :].T * scale)
        wq = wq.at[h, lo:lo + ROPE_DIM, KV_RANK:KV_RANK + ROPE_DIM].set(eye * scale)
    wuv = jnp.transpose(w_uv, (1, 2, 0))
    return wq.astype(BF16), wuv.astype(BF16)


def _rope_inv_freq():
    half = ROPE_HALF
    inv_freq = jnp.power(ROPE_THETA, -jnp.arange(half, dtype=F32) / half)
    lane = jnp.arange(LANES)
    row = jnp.where(lane % HEAD_DIM < ROPE_DIM, inv_freq[lane % half], 0.0)
    return row.reshape(1, LANES).astype(F32)


def kernel(x, c, positions, ada_w, ada_b, norm1_g, ffn1_w_in, ffn1_w_out, norm2_g, w_in, kv_norm_g,
           w_uk, w_uv, w_branch_a, w_branch_b, w_out, norm3_g, ffn2_w_in, ffn2_w_out, final_g):
    depth = ada_w.shape[0]
    pos3 = positions[..., None]
    invf = _rope_inv_freq()
    for l in range(depth):
        mod = _adaln(c, ada_w[l], ada_b[l])
        x = _ffn(x, mod, norm1_g[l], ffn1_w_in[l].astype(BF16), ffn1_w_out[l].astype(BF16), final_g,
                 mod_row=0, final=False)
        w_proj, w_gate = _pack_proj_weights(w_in[l])
        wq, wuv = _pack_dsa_weights(w_uk[l], w_uv[l])
        qa, qi, qb, kb, vt, kc, ct, ki, wt = _proj(x, pos3, mod, norm2_g[l], w_proj, kv_norm_g[l], invf)
        oa = _dsa(qa, qi, wt, kc, ct, ki, wq, wuv)
        ob = _moba(qb, kb, vt)
        x = _merge(x, oa, ob, mod, norm2_g[l], w_gate, w_branch_a[l].astype(BF16),
                   w_branch_b[l].astype(BF16), w_out[l].astype(BF16))
        last = l == depth - 1
        x = _ffn(x, mod, norm3_g[l], ffn2_w_in[l].astype(BF16), ffn2_w_out[l].astype(BF16), final_g,
                 mod_row=6, final=last)
    return x
```
